```python
import math
import jax
import jax.numpy as jnp
from jax import lax
import numpy as np

D_MODEL = 4096
BATCH = 4
SEQ = 2048
DEPTH = 2
DEC_BATCH = 8
DEC_SEQ = 4
PAST_LEN = 16384
PAGE_SIZE = 128

A_WIDTH = D_MODEL // 4
A_GROUP_DIM = 128
A_GROUPS = A_WIDTH // A_GROUP_DIM
A_CHUNK = 128
B_HEAD_DIM = 128
B_WIDTH = D_MODEL // 4
B_HEADS = B_WIDTH // B_HEAD_DIM
B_CHUNK = 16
C_HEAD_DIM = 128
C_WIDTH = D_MODEL // 2
C_HEADS = C_WIDTH // C_HEAD_DIM
MOBA_BLOCK = 256
MOBA_TOPK = 3
MOBA_Q_CHUNK = 8
N_BUCKETS = 32
MAX_DISTANCE = 128
N_BRANCHES = 3
IN_WIDTH = 2 * A_WIDTH + 4 * B_WIDTH + 3 * C_WIDTH + N_BRANCHES * D_MODEL
PEER_HEADS = 8
PEER_NKEYS = 128
PEER_EXPERTS = PEER_NKEYS * PEER_NKEYS
PEER_DKEY = 256
PEER_TOPK = 16
PEER_TOKEN_CHUNK = 128
EPS = 1e-6
NEG_BIG = -1e30
LB_FLOOR = 1e-30

kernel_name = "hybrid_gated_branch_decoder_step"


def rmsnorm(x, gain):
    xf = x.astype(jnp.float32)
    y = xf * lax.rsqrt(jnp.mean(xf * xf, axis=-1, keepdims=True) + EPS)
    return y.astype(x.dtype) * gain


def layernorm(x, gain, bias):
    xf = x.astype(jnp.float32)
    mu = jnp.mean(xf, axis=-1, keepdims=True)
    var = jnp.mean(jnp.square(xf - mu), axis=-1, keepdims=True)
    return ((xf - mu) * lax.rsqrt(var + EPS)).astype(x.dtype) * gain + bias


def pad_axis(x, axis, pad, mode="constant"):
    if pad == 0:
        return x
    widths = [(0, 0)] * x.ndim
    widths[axis] = (0, pad)
    return jnp.pad(x, widths, mode=mode)


def chunk_spatial_gating(u, v, w_s, b_s):
    bsz, t, _ = u.shape
    L = min(A_CHUNK, t)
    n = -(-t // L)
    u = pad_axis(u, 1, n * L - t).reshape(bsz, n, L, A_GROUPS, A_GROUP_DIM)
    v = pad_axis(v, 1, n * L - t).reshape(bsz, n, L, A_GROUPS, A_GROUP_DIM)
    causal = jnp.tril(jnp.ones((L, L), dtype=bool))
    w = jnp.where(causal[None], w_s[:, :L, :L], 0)
    mixed = jnp.einsum("gts,bnsgc->bntgc", w, v) + b_s[:, :L].T[None, None, :, :, None]
    return (u * mixed).reshape(bsz, n * L, A_WIDTH)[:, :t]


def hgrn2_chunked(q, log_f, k, i, S0):
    bsz, t, nh, dk = q.shape
    dv = i.shape[-1]
    L = min(B_CHUNK, t)
    n = -(-t // L)
    pad = n * L - t
    q, log_f, k, i = (pad_axis(z, 1, pad).reshape(bsz, n, L, nh, z.shape[-1]) for z in (q, log_f, k, i))
    b = jnp.cumsum(log_f, axis=2)
    causal = jnp.tril(jnp.ones((L, L), dtype=bool))[None, None, :, :, None, None]
    decay = jnp.exp(jnp.where(causal, b[:, :, :, None] - b[:, :, None, :], NEG_BIG))
    scores = jnp.einsum("bntshd,bnthd,bnshd->bnhts", decay, q, k)
    intra = jnp.einsum("bnhts,bnshv->bnthv", scores, i)
    b_last = b[:, :, -1]
    upd = jnp.einsum("bnlhd,bnlhv->bnhdv", k * jnp.exp(b_last[:, :, None] - b), i)

    def step(S, xs):
        dec, u = xs
        return dec[..., None] * S + u, S

    S_fin, S_prev = lax.scan(step, S0.astype(jnp.float32),
                             (jnp.exp(b_last).swapaxes(0, 1), upd.swapaxes(0, 1)))
    inter = jnp.einsum("bnlhd,bnhdv->bnlhv", q * jnp.exp(b), S_prev.swapaxes(0, 1))
    o = (intra + inter).reshape(bsz, n * L, nh, dv)[:, :t]
    return o.astype(i.dtype), S_fin.astype(S0.dtype)


def t5_bucket(rel):
    n = jnp.maximum(rel, 0)
    max_exact = N_BUCKETS // 2
    nf = jnp.maximum(n, max_exact).astype(jnp.float32)
    large = max_exact + (jnp.log(nf / max_exact) / math.log(MAX_DISTANCE / max_exact)
                         * (N_BUCKETS - max_exact)).astype(jnp.int32)
    large = jnp.minimum(large, N_BUCKETS - 1)
    return jnp.where(n < max_exact, n, large)


def moba_attention(q, k_all, v_all, q_pos, rel_bias):
    bsz, t, nh, d = q.shape
    tk = k_all.shape[1]
    nblk = -(-tk // MOBA_BLOCK)
    k_pad = pad_axis(k_all, 1, nblk * MOBA_BLOCK - tk)
    v_pad = pad_axis(v_all, 1, nblk * MOBA_BLOCK - tk)
    kb = k_pad.reshape(bsz, nblk, MOBA_BLOCK, nh, d)
    vb = v_pad.reshape(bsz, nblk, MOBA_BLOCK, nh, d)
    k_mean = jnp.mean(kb.astype(jnp.float32), axis=2)
    topk = min(MOBA_TOPK, nblk)
    qc_len = min(MOBA_Q_CHUNK, t)
    nq = -(-t // qc_len)
    qs = pad_axis(q, 1, nq * qc_len - t).reshape(bsz, nq, qc_len, nh, d).swapaxes(0, 1)
    ps = pad_axis(q_pos, 0, nq * qc_len - t, mode="edge").reshape(nq, qc_len)
    scale = d ** -0.5
    b_ix = jnp.arange(bsz)[:, None, None, None]
    h_ix = jnp.arange(nh)[None, None, :, None]
    offs = jnp.arange(MOBA_BLOCK, dtype=jnp.int32)

    def one_chunk(args):
        qq, pp = args
        own = pp // MOBA_BLOCK
        gate = jnp.einsum("bthd,bnhd->bthn", qq.astype(jnp.float32), k_mean)
        past = jnp.arange(nblk)[None, :] < own[:, None]
        gate = jnp.where(past[None, :, None, :], gate, NEG_BIG)
        _, sel = lax.top_k(gate, topk)
        valid = jnp.arange(topk)[None, :] < own[:, None]
        kg = kb[b_ix, sel, :, h_ix]
        vg = vb[b_ix, sel, :, h_ix]
        kpos = sel[..., None] * MOBA_BLOCK + offs
        bias_p = rel_bias[t5_bucket(pp[None, :, None, None, None] - kpos), h_ix[..., None]]
        lp = jnp.einsum("bthd,bthksd->bthks", qq, kg) * scale + bias_p
        lp = jnp.where(valid[None, :, None, :, None], lp, NEG_BIG).reshape(bsz, qc_len, nh, topk * MOBA_BLOCK)
        opos = own[:, None] * MOBA_BLOCK + offs
        ko = k_pad[:, opos]
        vo = v_pad[:, opos]
        bias_o = rel_bias[t5_bucket(pp[:, None] - opos)].transpose(0, 2, 1)[None]
        lo = jnp.einsum("bthd,btshd->bths", qq, ko) * scale + bias_o
        lo = jnp.where((opos <= pp[:, None])[None, :, None, :], lo, NEG_BIG)
        probs = jax.nn.softmax(jnp.concatenate([lp, lo], axis=-1).astype(jnp.float32), axis=-1)
        p_past = probs[..., :topk * MOBA_BLOCK].reshape(bsz, qc_len, nh, topk, MOBA_BLOCK).astype(q.dtype)
        p_own = probs[..., topk * MOBA_BLOCK:].astype(q.dtype)
        out = jnp.einsum("bthks,bthksd->bthd", p_past, vg) + jnp.einsum("bths,btshd->bthd", p_own, vo)
        return out.astype(q.dtype)

    out = lax.map(one_chunk, (qs, ps))
    return out.swapaxes(0, 1).reshape(bsz, nq * qc_len, nh, d)[:, :t]


def peer_ffn(h, wq, keys1, keys2, u_tab, v_tab):
    bsz, t, d = h.shape
    n_tok = bsz * t
    C = min(PEER_TOKEN_CHUNK, n_tok)
    nc = -(-n_tok // C)
    xs = pad_axis(h.reshape(n_tok, d), 0, nc * C - n_tok).reshape(nc, C, d)
    half = PEER_DKEY // 2

    def retrieve(xc):
        q = (xc @ wq).reshape(C, PEER_HEADS, 2, half).astype(jnp.float32)
        s1 = jnp.einsum("chd,hkd->chk", q[:, :, 0], keys1.astype(jnp.float32))
        s2 = jnp.einsum("chd,hkd->chk", q[:, :, 1], keys2.astype(jnp.float32))
        v1, i1 = lax.top_k(s1, PEER_TOPK)
        v2, i2 = lax.top_k(s2, PEER_TOPK)
        cand = (v1[..., :, None] + v2[..., None, :]).reshape(C, PEER_HEADS, PEER_TOPK * PEER_TOPK)
        cidx = (i1[..., :, None] * PEER_NKEYS + i2[..., None, :]).reshape(C, PEER_HEADS, PEER_TOPK * PEER_TOPK)
        best, pos = lax.top_k(cand, PEER_TOPK)
        eidx = jnp.take_along_axis(cidx, pos, axis=-1)
        g = jax.nn.softmax(best, axis=-1).astype(xc.dtype)
        u_e = u_tab[eidx]
        v_e = v_tab[eidx]
        act = jax.nn.gelu(jnp.einsum("cd,chkd->chk", xc, u_e))
        return jnp.einsum("chk,chkd->cd", g * act, v_e)

    y = lax.map(retrieve, xs).reshape(nc * C, d)[:n_tok]
    return y.reshape(bsz, t, d)


def token_mixers(p, l, lb, h, S0, k_past, v_past, q_pos):
    bsz, t, _ = h.shape
    proj = h @ p["w_in"][l]
    sizes = (A_WIDTH, A_WIDTH, B_WIDTH, B_WIDTH, B_WIDTH, B_WIDTH, C_WIDTH, C_WIDTH, C_WIDTH)
    cuts = [int(c) for c in np.cumsum(sizes)]
    a_u, a_v, b_q, b_f, b_i, b_g, c_q, c_k, c_v, g_pre = jnp.split(proj, cuts, axis=-1)
    a_u = jax.nn.gelu(a_u)
    a_v = layernorm(jax.nn.gelu(a_v), p["ln_v_gain"][l], p["ln_v_bias"][l])
    o_a = chunk_spatial_gating(a_u, a_v, p["w_spatial"][l], p["b_spatial"][l])
    hb = lambda z: z.reshape(bsz, t, B_HEADS, B_HEAD_DIM)
    f_pre = hb(b_f).astype(jnp.float32)
    log_f = jnp.logaddexp(jax.nn.log_sigmoid(f_pre),
                          jnp.log(jnp.maximum(lb, LB_FLOOR)) + jax.nn.log_sigmoid(-f_pre))
    k_in = (1.0 - lb) * jax.nn.sigmoid(-f_pre)
    o_b, S_fin = hgrn2_chunked(jax.nn.silu(hb(b_q)), log_f, k_in, hb(b_i), S0)
    o_b = (rmsnorm(o_b, p["hgrn_norm"][l]) * jax.nn.silu(hb(b_g))).reshape(bsz, t, B_WIDTH)
    hc = lambda z: z.reshape(bsz, t, C_HEADS, C_HEAD_DIM)
    q, k_new, v_new = hc(c_q), hc(c_k), hc(c_v)
    k_all = k_new if k_past is None else jnp.concatenate([k_past, k_new], axis=1)
    v_all = v_new if v_past is None else jnp.concatenate([v_past, v_new], axis=1)
    o_c = moba_attention(q, k_all, v_all, q_pos, p["rel_bias"]).reshape(bsz, t, C_WIDTH)
    gates = jax.nn.sigmoid(g_pre.astype(jnp.float32)).astype(h.dtype).reshape(bsz, t, N_BRANCHES, D_MODEL)
    merged = (gates[:, :, 0] * (o_a @ p["w_branch_a"][l])
              + gates[:, :, 1] * (o_b @ p["w_branch_b"][l])
              + gates[:, :, 2] * (o_c @ p["w_branch_c"][l]))
    return merged @ p["w_out"][l], a_v, k_new, v_new, S_fin


def trunk_layer(p, l, lb, x, c, S0, k_past, v_past, q_pos):
    mod = jax.nn.silu(c) @ p["w_ada"][l] + p["b_ada"][l]
    sh1, sc1, g1, sh2, sc2, g2 = jnp.split(mod[:, None, :], 6, axis=-1)
    h = rmsnorm(x, p["norm_mix"][l]) * (1.0 + sc1) + sh1
    mix, a_v, k_new, v_new, S_fin = token_mixers(p, l, lb, h, S0, k_past, v_past, q_pos)
    x = x + g1 * mix
    h = rmsnorm(x, p["norm_ffn"][l]) * (1.0 + sc2) + sh2
    x = x + g2 * peer_ffn(h, p["peer_wq"][l], p["peer_keys1"][l], p["peer_keys2"][l],
                          p["peer_u"][l], p["peer_v"][l])
    return x, a_v, k_new, v_new, S_fin


def setup_inputs(seed: int = 0) -> dict:
    key = jax.random.key(seed)
    ks = jax.random.split(key, 32)

    def nrm(k, shape, scale):
        return scale * jax.random.normal(k, shape, jnp.float32)

    n_pages = PAST_LEN // PAGE_SIZE
    n_used = DEC_BATCH * n_pages
    n_phys = n_used + max(1, n_used // 4)
    page_table = jax.random.permutation(ks[5], n_phys)[:n_used].reshape(DEC_BATCH, n_pages).astype(jnp.int32)
    return {
        "x_prompt": nrm(ks[0], (BATCH, SEQ, D_MODEL), 1.0),
        "x_sample": nrm(ks[1], (DEC_BATCH, DEC_SEQ, D_MODEL), 1.0),
        "cache_k": nrm(ks[2], (DEPTH, n_phys, PAGE_SIZE, C_HEADS, C_HEAD_DIM), 1.0),
        "cache_v": nrm(ks[3], (DEPTH, n_phys, PAGE_SIZE, C_HEADS, C_HEAD_DIM), 1.0),
        "page_table": page_table,
        "state_hgrn": nrm(ks[4], (DEPTH, DEC_BATCH, B_HEADS, B_HEAD_DIM, B_HEAD_DIM), 0.5),
        "c_prompt": nrm(ks[6], (BATCH, D_MODEL), 1.0),
        "c_sample": nrm(ks[7], (DEC_BATCH, D_MODEL), 1.0),
        "w_ada": nrm(ks[8], (DEPTH, D_MODEL, 6 * D_MODEL), 0.5 * D_MODEL ** -0.5),
        "b_ada": nrm(ks[9], (DEPTH, 6 * D_MODEL), 0.02),
        "norm_mix": 1.0 + nrm(ks[10], (DEPTH, D_MODEL), 0.02),
        "norm_ffn": 1.0 + nrm(ks[11], (DEPTH, D_MODEL), 0.02),
        "norm_final": 1.0 + nrm(ks[12], (D_MODEL,), 0.02),
        "w_in": nrm(ks[13], (DEPTH, D_MODEL, IN_WIDTH), D_MODEL ** -0.5),
        "ln_v_gain": 1.0 + nrm(ks[14], (DEPTH, A_WIDTH), 0.02),
        "ln_v_bias": nrm(ks[15], (DEPTH, A_WIDTH), 0.02),
        "w_spatial": nrm(ks[16], (DEPTH, A_GROUPS, A_CHUNK, A_CHUNK), A_CHUNK ** -0.5),
        "b_spatial": 1.0 + nrm(ks[17], (DEPTH, A_GROUPS, A_CHUNK), 0.02),
        "hgrn_gamma": nrm(ks[18], (DEPTH, B_WIDTH), 0.5),
        "hgrn_norm": 1.0 + nrm(ks[19], (DEPTH, B_HEAD_DIM), 0.02),
        "rel_bias": nrm(ks[20], (N_BUCKETS, C_HEADS), 0.5),
        "w_branch_a": nrm(ks[21], (DEPTH, A_WIDTH, D_MODEL), A_WIDTH ** -0.5),
        "w_branch_b": nrm(ks[22], (DEPTH, B_WIDTH, D_MODEL), B_WIDTH ** -0.5),
        "w_branch_c": nrm(ks[23], (DEPTH, C_WIDTH, D_MODEL), C_WIDTH ** -0.5),
        "w_out": nrm(ks[24], (DEPTH, D_MODEL, D_MODEL), D_MODEL ** -0.5),
        "peer_wq": nrm(ks[25], (DEPTH, D_MODEL, PEER_HEADS * PEER_DKEY), D_MODEL ** -0.5),
        "peer_keys1": nrm(ks[26], (DEPTH, PEER_HEADS, PEER_NKEYS, PEER_DKEY // 2), (PEER_DKEY // 2) ** -0.5),
        "peer_keys2": nrm(ks[27], (DEPTH, PEER_HEADS, PEER_NKEYS, PEER_DKEY // 2), (PEER_DKEY // 2) ** -0.5),
        "peer_u": nrm(ks[28], (DEPTH, PEER_EXPERTS, D_MODEL), D_MODEL ** -0.5),
        "peer_v": nrm(ks[29], (DEPTH, PEER_EXPERTS, D_MODEL), 0.5),
    }


def reference(x_prompt, x_sample, cache_k, cache_v, page_table, state_hgrn, c_prompt, c_sample,
              w_ada, b_ada, norm_mix, norm_ffn, norm_final, w_in, ln_v_gain, ln_v_bias,
              w_spatial, b_spatial, hgrn_gamma, hgrn_norm, rel_bias, w_branch_a, w_branch_b,
              w_branch_c, w_out, peer_wq, peer_keys1, peer_keys2, peer_u, peer_v):
    p = dict(w_ada=w_ada, b_ada=b_ada, norm_mix=norm_mix, norm_ffn=norm_ffn, w_in=w_in,
             ln_v_gain=ln_v_gain, ln_v_bias=ln_v_bias, w_spatial=w_spatial, b_spatial=b_spatial,
             hgrn_norm=hgrn_norm, rel_bias=rel_bias, w_branch_a=w_branch_a, w_branch_b=w_branch_b,
             w_branch_c=w_branch_c, w_out=w_out, peer_wq=peer_wq, peer_keys1=peer_keys1,
             peer_keys2=peer_keys2, peer_u=peer_u, peer_v=peer_v)
    gam = jax.nn.softmax(hgrn_gamma.astype(jnp.float32), axis=0)
    lower_bounds = (jnp.cumsum(gam, axis=0) - gam[:1]).reshape(DEPTH, B_HEADS, B_HEAD_DIM)
    n_dec = x_sample.shape[0]
    past_len = page_table.shape[1] * cache_k.shape[2]
    pos_prompt = jnp.arange(x_prompt.shape[1], dtype=jnp.int32)
    pos_sample = past_len + jnp.arange(x_sample.shape[1], dtype=jnp.int32)
    s0_prompt = jnp.zeros((x_prompt.shape[0], B_HEADS, B_HEAD_DIM, B_HEAD_DIM), x_prompt.dtype)
    xp, xs = x_prompt, x_sample
    k_rows_p, v_rows_p, k_rows_s, v_rows_s, hgrn_p, hgrn_s, chunk_v_s = [], [], [], [], [], [], []
    for l in range(DEPTH):
        xp, _, k_p, v_p, s_p = trunk_layer(p, l, lower_bounds[l], xp, c_prompt, s0_prompt,
                                           None, None, pos_prompt)
        k_past = cache_k[l][page_table].reshape(n_dec, past_len, C_HEADS, C_HEAD_DIM)
        v_past = cache_v[l][page_table].reshape(n_dec, past_len, C_HEADS, C_HEAD_DIM)
        xs, av_s, k_s, v_s, s_s = trunk_layer(p, l, lower_bounds[l], xs, c_sample, state_hgrn[l],
                                              k_past, v_past, pos_sample)
        k_rows_p.append(k_p)
        v_rows_p.append(v_p)
        k_rows_s.append(k_s)
        v_rows_s.append(v_s)
        hgrn_p.append(s_p)
        hgrn_s.append(s_s)
        chunk_v_s.append(av_s)
    y_prompt = rmsnorm(xp, norm_final)
    y_sample = rmsnorm(xs, norm_final)
    return (y_prompt, y_sample, jnp.stack(k_rows_p), jnp.stack(v_rows_p), jnp.stack(k_rows_s),
            jnp.stack(v_rows_s), jnp.stack(hgrn_p), jnp.stack(hgrn_s), jnp.stack(chunk_v_s))
```

```python
import functools
import math

import numpy as np
import jax
import jax.numpy as jnp
from jax import lax
from jax.experimental import pallas as pl
from jax.experimental.pallas import tpu as pltpu

F32 = jnp.float32
BF16 = jnp.bfloat16

EPS = 1e-6
NEG_BIG = -1e30
LB_FLOOR = 1e-30

GROUP_DIM = 128
MOBA_BLOCK = 256
MOBA_TOPK = 3
N_BUCKETS = 32
MAX_DISTANCE = 128
PEER_TOPK = 16
HGRN_SUB = 16
SAMPLE_ROWS = 16

VMEM_LIMIT_V7X = 56 * 1024 * 1024


def _params(semantics):
    return pltpu.CompilerParams(dimension_semantics=semantics, vmem_limit_bytes=VMEM_LIMIT_V7X)


def _gelu(x):
    return jax.nn.gelu(x)


def _sigmoid(x):
    return jax.nn.sigmoid(x)


def _group_spec(arr, tm, rows_per_group, ncol_block, col_of):
    g, r, _ = arr.shape
    if r == 1:
        return pl.BlockSpec((None, 1, ncol_block), lambda *ix: ((ix[0] * tm) // rows_per_group, 0, col_of(*ix)))
    assert g == 1
    return pl.BlockSpec((None, tm, ncol_block), lambda *ix: (0, ix[0], col_of(*ix)))


def _rownorm_kernel(*refs, has_res, has_mod):
    it = iter(refs)
    x_ref = next(it)
    if has_res:
        y_ref = next(it)
        g_ref = next(it)
    gain_ref = next(it)
    if has_mod:
        sc_ref = next(it)
        sh_ref = next(it)
    if has_res:
        xo_ref = next(it)
    h_ref = next(it)
    x = x_ref[...]
    if has_res:
        x = x + g_ref[...] * y_ref[...]
        xo_ref[...] = x
    h = x * lax.rsqrt(jnp.mean(x * x, axis=-1, keepdims=True) + EPS) * gain_ref[...]
    if has_mod:
        h = h * (1.0 + sc_ref[...]) + sh_ref[...]
    h_ref[...] = h.astype(h_ref.dtype)


def _rownorm(x, gain, *, sc=None, sh=None, y=None, gate=None, rows_per_group, out_dtype, tm=256):
    m, d = x.shape
    tm = min(tm, m)
    has_res, has_mod = y is not None, sc is not None
    row = pl.BlockSpec((tm, d), lambda i: (i, 0))
    zero = lambda i: 0
    ins, specs = [x], [row]
    if has_res:
        ins += [y, gate]
        specs += [row, _group_spec(gate, tm, rows_per_group, d, zero)]
    ins.append(gain.reshape(1, d))
    specs.append(pl.BlockSpec((1, d), lambda i: (0, 0)))
    if has_mod:
        ins += [sc, sh]
        specs += [_group_spec(sc, tm, rows_per_group, d, zero), _group_spec(sh, tm, rows_per_group, d, zero)]
    out_shape, out_specs = [], []
    if has_res:
        out_shape.append(jax.ShapeDtypeStruct((m, d), F32))
        out_specs.append(row)
    out_shape.append(jax.ShapeDtypeStruct((m, d), out_dtype))
    out_specs.append(row)
    res = pl.pallas_call(
        functools.partial(_rownorm_kernel, has_res=has_res, has_mod=has_mod),
        grid=(m // tm,), in_specs=specs, out_specs=out_specs, out_shape=out_shape,
        compiler_params=_params(("parallel",)), name="rownorm",
    )(*ins)
    return (res[0], res[1]) if has_res else (None, res[0])


def _mm_kernel(*refs, nk, pre_silu, has_bias, has_res):
    it = iter(refs)
    x_ref = next(it)
    w_ref = next(it)
    b_ref = next(it) if has_bias else None
    if has_res:
        r_ref = next(it)
        g_ref = next(it)
    o_ref = next(it)
    acc_ref = next(it) if nk > 1 else None
    x = x_ref[...]
    if pre_silu:
        x = x * _sigmoid(x)
    part = jnp.dot(x.astype(BF16), w_ref[...].astype(BF16), preferred_element_type=F32)

    def finish(acc):
        if has_bias:
            acc = acc + b_ref[...]
        if has_res:
            acc = r_ref[...] + g_ref[...] * acc
        o_ref[...] = acc.astype(o_ref.dtype)

    if nk == 1:
        finish(part)
    else:
        k = pl.program_id(2)

        @pl.when(k == 0)
        def _():
            acc_ref[...] = part

        @pl.when(k > 0)
        def _():
            acc_ref[...] += part

        @pl.when(k == nk - 1)
        def _():
            finish(acc_ref[...])


def _matmul(x, w, l, *, bias=None, resid=None, gate=None, rows_per_group=None, pre_silu=False,
            out_dtype=F32, tm=1024, tn=512, tk=None):
    m, kdim = x.shape
    n = w.shape[2]
    tm, tn = min(tm, m), min(tn, n)
    tk = kdim if tk is None else min(tk, kdim)
    nk = kdim // tk
    has_bias, has_res = bias is not None, resid is not None
    ins = [x, w]
    specs = [pl.BlockSpec((tm, tk), lambda i, j, k: (i, k)),
             pl.BlockSpec((None, tk, tn), lambda i, j, k: (l, k, j))]
    if has_bias:
        ins.append(bias.reshape(bias.shape[0], 1, n))
        specs.append(pl.BlockSpec((None, 1, tn), lambda i, j, k: (l, 0, j)))
    if has_res:
        ins += [resid, gate]
        specs += [pl.BlockSpec((tm, tn), lambda i, j, k: (i, j)),
                  _group_spec(gate, tm, rows_per_group, tn, lambda i, j, k: j)]
    return pl.pallas_call(
        functools.partial(_mm_kernel, nk=nk, pre_silu=pre_silu, has_bias=has_bias, has_res=has_res),
        grid=(m // tm, n // tn, nk), in_specs=specs,
        out_specs=pl.BlockSpec((tm, tn), lambda i, j, k: (i, j)),
        out_shape=jax.ShapeDtypeStruct((m, n), out_dtype),
        scratch_shapes=[pltpu.VMEM((tm, tn), F32)] if nk > 1 else [],
        compiler_params=_params(("parallel", "parallel", "arbitrary")), name="matmul",
    )(*ins)


def _branch_a_kernel(u_ref, v_ref, w_ref, b_ref, lg_ref, lb_ref, o_ref, *av_ref, groups):
    u = _gelu(u_ref[...])
    v = _gelu(v_ref[...])
    mu = jnp.mean(v, axis=-1, keepdims=True)
    vc = v - mu
    var = jnp.mean(vc * vc, axis=-1, keepdims=True)
    v = vc * lax.rsqrt(var + EPS) * lg_ref[...] + lb_ref[...]
    if av_ref:
        av_ref[0][...] = v
    vb = v.astype(BF16)
    for g in range(groups):
        cols = slice(g * GROUP_DIM, (g + 1) * GROUP_DIM)
        mixed = jnp.dot(w_ref[g].astype(BF16), vb[:, cols], preferred_element_type=F32) + b_ref[:, cols]
        o_ref[:, cols] = (u[:, cols] * mixed).astype(o_ref.dtype)


def _branch_a(proj, w_eff, b_eff, ln_gain, ln_bias, *, width, want_av):
    m = proj.shape[0]
    groups, chunk, _ = w_eff.shape
    full = lambda i: (0, 0)
    out_shape = [jax.ShapeDtypeStruct((m, width), BF16)]
    out_specs = [pl.BlockSpec((chunk, width), lambda i: (i, 0))]
    if want_av:
        out_shape.append(jax.ShapeDtypeStruct((m, width), F32))
        out_specs.append(pl.BlockSpec((chunk, width), lambda i: (i, 0)))
    return pl.pallas_call(
        functools.partial(_branch_a_kernel, groups=groups),
        grid=(m // chunk,),
        in_specs=[pl.BlockSpec((chunk, width), lambda i: (i, 0)), pl.BlockSpec((chunk, width), lambda i: (i, 1)),
                  pl.BlockSpec((groups, chunk, chunk), lambda i: (0, 0, 0)), pl.BlockSpec((chunk, width), full),
                  pl.BlockSpec((1, width), full), pl.BlockSpec((1, width), full)],
        out_specs=out_specs, out_shape=out_shape,
        compiler_params=_params(("parallel",)), name="branch_a",
    )(proj, proj, w_eff, b_eff, ln_gain.reshape(1, width), ln_bias.reshape(1, width))


def _hgrn_kernel(q_ref, f_ref, i_ref, g_ref, lb_ref, s0_ref, gain_ref, o_ref, sfin_ref,
                 st_s, q_s, k_s, b_s, o_s, *, chunk, t_valid):
    c = pl.program_id(2)

    @pl.when(c == 0)
    def _():
        st_s[...] = s0_ref[...].T

    z = f_ref[...]
    lb = lb_ref[...]
    soft = jnp.log1p(jnp.exp(-jnp.abs(z)))
    ls_pos = jnp.minimum(z, 0.0) - soft
    ls_neg = jnp.minimum(-z, 0.0) - soft
    t2 = jnp.log(jnp.maximum(lb, LB_FLOOR)) + ls_neg
    log_f = jnp.maximum(ls_pos, t2) + jnp.log1p(jnp.exp(-jnp.abs(ls_pos - t2)))
    k_in = (1.0 - lb) * _sigmoid(-z)
    row = lax.broadcasted_iota(jnp.int32, z.shape, 0)
    sub = row % HGRN_SUB
    if t_valid is not None:
        live = row < t_valid
        log_f = jnp.where(live, log_f, 0.0)
        k_in = jnp.where(live, k_in, 0.0)
    b = log_f
    step = 1
    while step < HGRN_SUB:
        b = b + jnp.where(sub >= step, pltpu.roll(b, step, 0), 0.0)
        step *= 2
    qv = q_ref[...]
    q_s[...] = qv * _sigmoid(qv)
    k_s[...] = k_in
    b_s[...] = b

    ti = lax.broadcasted_iota(jnp.int32, (HGRN_SUB, HGRN_SUB, GROUP_DIM), 0)
    si = lax.broadcasted_iota(jnp.int32, (HGRN_SUB, HGRN_SUB, GROUP_DIM), 1)
    causal = si <= ti
    ones = jnp.ones((GROUP_DIM, GROUP_DIM), BF16)

    def block(j, carry):
        r = pl.ds(pl.multiple_of(j * HGRN_SUB, HGRN_SUB), HGRN_SUB)
        qb, kb, bb, ib = q_s[r, :], k_s[r, :], b_s[r, :], i_ref[r, :]
        bl = bb[HGRN_SUB - 1:HGRN_SUB, :]
        st = st_s[...]
        inter = lax.dot_general(qb * jnp.exp(bb), st, (((1,), (1,)), ((), ())), preferred_element_type=F32)
        dec = jnp.exp(jnp.where(causal, bb[:, None, :] - bb[None, :, :], NEG_BIG))
        a = (dec * qb[:, None, :]) * kb[None, :, :]
        rs = jnp.dot(a.reshape(HGRN_SUB * HGRN_SUB, GROUP_DIM).astype(BF16), ones,
                     preferred_element_type=F32)
        intra = jnp.sum(rs.reshape(HGRN_SUB, HGRN_SUB, GROUP_DIM) * ib[None, :, :], axis=1)
        o_s[r, :] = inter + intra
        ke = kb * jnp.exp(bl - bb)
        upd = lax.dot_general(ib, ke, (((0,), (0,)), ((), ())), preferred_element_type=F32)
        st_s[...] = st * jnp.exp(bl) + upd
        return carry

    lax.fori_loop(0, chunk // HGRN_SUB, block, 0)

    o = o_s[...]
    o = o * lax.rsqrt(jnp.mean(o * o, axis=-1, keepdims=True) + EPS) * gain_ref[...]
    gv = g_ref[...]
    o_ref[...] = (o * (gv * _sigmoid(gv))).astype(o_ref.dtype)

    @pl.when(c == pl.num_programs(2) - 1)
    def _():
        sfin_ref[...] = st_s[...].T


def _hgrn(proj, col0, lb, s0, norm_gain, *, n_req, t_len, t_valid, chunk):
    heads = lb.shape[0]
    width = heads * GROUP_DIM
    base = col0 // GROUP_DIM
    nch = t_len // chunk
    col = lambda off: pl.BlockSpec((chunk, GROUP_DIM), lambda b, h, c: (b * nch + c, base + off * heads + h))
    o, s_fin = pl.pallas_call(
        functools.partial(_hgrn_kernel, chunk=chunk, t_valid=t_valid),
        grid=(n_req, heads, nch),
        in_specs=[col(0), col(1), col(2), col(3),
                  pl.BlockSpec((None, 1, GROUP_DIM), lambda b, h, c: (h, 0, 0)),
                  pl.BlockSpec((None, None, GROUP_DIM, GROUP_DIM), lambda b, h, c: (b, h, 0, 0)),
                  pl.BlockSpec((1, GROUP_DIM), lambda b, h, c: (0, 0))],
        out_specs=[pl.BlockSpec((chunk, GROUP_DIM), lambda b, h, c: (b * nch + c, h)),
                   pl.BlockSpec((None, None, GROUP_DIM, GROUP_DIM), lambda b, h, c: (b, h, 0, 0))],
        out_shape=[jax.ShapeDtypeStruct((n_req * t_len, width), BF16),
                   jax.ShapeDtypeStruct((n_req, heads, GROUP_DIM, GROUP_DIM), F32)],
        scratch_shapes=[pltpu.VMEM((GROUP_DIM, GROUP_DIM), F32)] + [pltpu.VMEM((chunk, GROUP_DIM), F32)] * 4,
        compiler_params=_params(("parallel", "parallel", "arbitrary")), name="hgrn",
    )(proj, proj, proj, proj, lb.reshape(heads, 1, GROUP_DIM), s0, norm_gain.reshape(1, GROUP_DIM))
    return o, s_fin


def _t5_bucket_np(rel):
    n = np.maximum(rel, 0)
    max_exact = N_BUCKETS // 2
    nf = np.maximum(n, max_exact).astype(np.float32)
    large = max_exact + (np.log(nf / np.float32(max_exact)) / np.float32(math.log(MAX_DISTANCE / max_exact))
                         * np.float32(N_BUCKETS - max_exact)).astype(np.int32)
    large = np.minimum(large, N_BUCKETS - 1)
    return np.where(n < max_exact, n, large).astype(np.int32)


def _top_blocks(gate, n_past, n_keep):
    nblk = gate.shape[0]
    rid = lax.broadcasted_iota(jnp.int32, gate.shape, 0)
    g = jnp.where(rid < n_past, gate, NEG_BIG)
    sel = jnp.zeros(gate.shape, F32)
    for r in range(min(MOBA_TOPK, nblk)):
        mx = jnp.max(g, axis=0, keepdims=True)
        idx = jnp.min(jnp.where(g == mx, rid, nblk), axis=0, keepdims=True)
        pick = rid == idx
        live = jnp.logical_and(pick, jnp.full(gate.shape, r, jnp.int32) < n_keep)
        sel = jnp.where(live, 1.0, sel)
        g = jnp.where(pick, -jnp.inf, g)
    return sel


def _moba_prompt_kernel(q_ref, k_ref, v_ref, bo_ref, bp_ref, far_ref, o_ref, m_s, l_s, acc_s, *, nblk):
    i = pl.program_id(2)
    blk = MOBA_BLOCK
    scale = GROUP_DIM ** -0.5
    q = q_ref[...].astype(BF16)
    nt = (((1,), (1,)), ((), ()))

    def logits_t(kj):
        return lax.dot_general(kj.astype(BF16), q, nt, preferred_element_type=F32) * scale

    def pv_t(vj, p):
        return lax.dot_general(vj.astype(BF16), p.astype(BF16), (((0,), (0,)), ((), ())),
                               preferred_element_type=F32)

    k_mean = jnp.concatenate(
        [jnp.mean(k_ref[j * blk:(j + 1) * blk, :], axis=0, keepdims=True) for j in range(nblk)], axis=0)
    gate = lax.dot_general(k_mean, q_ref[...], nt, preferred_element_type=F32)
    sel = _top_blocks(gate, i, i)

    own = pl.ds(pl.multiple_of(i * blk, blk), blk)
    s = logits_t(k_ref[own, :]) + bo_ref[...]
    ks = lax.broadcasted_iota(jnp.int32, (blk, blk), 0)
    qs = lax.broadcasted_iota(jnp.int32, (blk, blk), 1)
    s = jnp.where(ks <= qs, s, NEG_BIG)
    m0 = jnp.max(s, axis=0, keepdims=True)
    p = jnp.exp(s - m0)
    m_s[...] = m0
    l_s[...] = jnp.sum(p, axis=0, keepdims=True)
    acc_s[...] = pv_t(v_ref[own, :], p)

    def past_block(j, bias):
        sj = logits_t(k_ref[j * blk:(j + 1) * blk, :]) + bias
        keep = sel[j:j + 1, :] > 0.0
        m_old = m_s[...]
        m_new = jnp.maximum(m_old, jnp.max(jnp.where(keep, sj, NEG_BIG), axis=0, keepdims=True))
        alpha = jnp.exp(m_old - m_new)
        pj = jnp.where(keep, jnp.exp(sj - m_new), 0.0)
        m_s[...] = m_new
        l_s[...] = l_s[...] * alpha + jnp.sum(pj, axis=0, keepdims=True)
        acc_s[...] = acc_s[...] * alpha + pv_t(v_ref[j * blk:(j + 1) * blk, :], pj)

    for j in range(nblk - 1):
        @pl.when(j == i - 1)
        def _(j=j):
            past_block(j, bp_ref[...])

        @pl.when(j < i - 1)
        def _(j=j):
            past_block(j, far_ref[...])

    o_ref[...] = (acc_s[...] / l_s[...]).T.astype(o_ref.dtype)


def _moba_bias_tiles(rel_bias):
    blk = MOBA_BLOCK
    s = np.arange(blk)[:, None]
    t = np.arange(blk)[None, :]
    own = jnp.transpose(rel_bias[_t5_bucket_np(t - s)], (2, 0, 1))
    prev = jnp.transpose(rel_bias[_t5_bucket_np(blk + t - s)], (2, 0, 1))
    far_bucket = _t5_bucket_np(np.array([blk + 1, 2 * blk, 1 << 20]))
    assert far_bucket.min() == far_bucket.max() == N_BUCKETS - 1
    far = jnp.broadcast_to(rel_bias[N_BUCKETS - 1][:, None, None], (rel_bias.shape[1], 1, blk))
    return own, prev, far


def _moba_prompt(proj, col0, bias_tiles, *, n_req, t_len, heads):
    blk = MOBA_BLOCK
    nblk = t_len // blk
    base = col0 // GROUP_DIM
    own, prev, far = bias_tiles
    seq = lambda off: pl.BlockSpec((t_len, GROUP_DIM), lambda b, h, i: (b, base + off * heads + h))
    tile = pl.BlockSpec((None, blk, blk), lambda b, h, i: (h, 0, 0))
    return pl.pallas_call(
        functools.partial(_moba_prompt_kernel, nblk=nblk),
        grid=(n_req, heads, nblk),
        in_specs=[pl.BlockSpec((blk, GROUP_DIM), lambda b, h, i: (b * nblk + i, base + h)), seq(1), seq(2),
                  tile, tile, pl.BlockSpec((None, 1, blk), lambda b, h, i: (h, 0, 0))],
        out_specs=pl.BlockSpec((blk, GROUP_DIM), lambda b, h, i: (b * nblk + i, h)),
        out_shape=jax.ShapeDtypeStruct((n_req * t_len, heads * GROUP_DIM), BF16),
        scratch_shapes=[pltpu.VMEM((1, blk), F32), pltpu.VMEM((1, blk), F32), pltpu.VMEM((GROUP_DIM, blk), F32)],
        compiler_params=_params(("parallel", "parallel", "arbitrary")), name="moba_prompt",
    )(proj, proj, proj, own, prev, far)


def _cache_logits_kernel(pt_ref, *refs, pages):
    k_refs, qbd_ref, s_ref, g_ref = refs[:pages], refs[pages], refs[pages + 1], refs[pages + 2]
    qbd = qbd_ref[...].astype(BF16)
    psz = k_refs[0].shape[0]
    tot = None
    for p in range(pages):
        s = jnp.dot(k_refs[p][...].astype(BF16), qbd, preferred_element_type=F32)
        s_ref[p * psz:(p + 1) * psz, :] = s
        part = jnp.sum(s, axis=0, keepdims=True)
        tot = part if tot is None else tot + part
    g_ref[...] = tot * (1.0 / (pages * psz))


def _cache_select_kernel(s_ref, g_ref, knew_ref, qbd_ref, blast_ref, far_ref, bown_ref, mown_ref,
                         p_ref, pown_ref, *, nblk):
    blk = MOBA_BLOCK
    scale = GROUP_DIM ** -0.5
    sel = _top_blocks(g_ref[...], nblk, nblk)
    lo = jnp.dot(knew_ref[...].astype(BF16), qbd_ref[...].astype(BF16), preferred_element_type=F32)
    lo = jnp.where(mown_ref[...] > 0.0, lo * scale + bown_ref[...], NEG_BIG)
    far = far_ref[...]

    def logits(j):
        bias = blast_ref[...] if j == nblk - 1 else far
        return jnp.where(sel[j:j + 1, :] > 0.0, s_ref[j * blk:(j + 1) * blk, :] * scale + bias, NEG_BIG)

    mx = jnp.max(lo, axis=0, keepdims=True)
    for j in range(nblk):
        mx = jnp.maximum(mx, jnp.max(logits(j), axis=0, keepdims=True))
    den = jnp.sum(jnp.exp(lo - mx), axis=0, keepdims=True)
    for j in range(nblk):
        den = den + jnp.sum(jnp.exp(logits(j) - mx), axis=0, keepdims=True)
    inv = 1.0 / den
    pown_ref[...] = jnp.exp(lo - mx) * inv
    for j in range(nblk):
        p_ref[j * blk:(j + 1) * blk, :] = jnp.exp(logits(j) - mx) * inv


def _cache_pv_kernel(pt_ref, *refs, pages):
    v_refs, p_ref, pown_ref, vnew_ref, o_ref, acc_s = (refs[:pages], refs[pages], refs[pages + 1],
                                                       refs[pages + 2], refs[pages + 3], refs[pages + 4])
    j = pl.program_id(1)
    psz = v_refs[0].shape[0]
    ta = (((0,), (0,)), ((), ()))

    @pl.when(j == 0)
    def _():
        acc_s[...] = lax.dot_general(pown_ref[...].astype(BF16), vnew_ref[...].astype(BF16), ta,
                                     preferred_element_type=F32)

    tot = acc_s[...]
    for p in range(pages):
        tot = tot + lax.dot_general(p_ref[p * psz:(p + 1) * psz, :].astype(BF16), v_refs[p][...].astype(BF16),
                                    ta, preferred_element_type=F32)
    acc_s[...] = tot

    @pl.when(j == pl.num_programs(1) - 1)
    def _():
        o_ref[...] = acc_s[...]


def _moba_sample(proj_s, col0, cache_k, cache_v, l, page_table, rel_bias, *, heads, t_valid):
    n_req, n_pages = page_table.shape
    depth, n_phys, psz = cache_k.shape[:3]
    width = heads * GROUP_DIM
    assert col0 % width == 0
    lanes = heads * 8
    assert t_valid <= 8 and lanes == GROUP_DIM
    past = n_pages * psz
    blk = MOBA_BLOCK
    nblk = past // blk
    assert past % blk == 0 and t_valid <= blk
    ppb = blk // psz
    rows = SAMPLE_ROWS

    q = proj_s[:, col0:col0 + width].reshape(n_req, rows, heads, GROUP_DIM)[:, :8]
    eye = jnp.eye(heads, dtype=F32)
    qbd = jnp.einsum("bthd,hg->bhdgt", q, eye).reshape(n_req, width, lanes)
    ck = cache_k.reshape(depth, n_phys, psz, width)
    cv = cache_v.reshape(depth, n_phys, psz, width)
    pt = page_table.reshape(-1).astype(jnp.int32)

    page = lambda p: pl.BlockSpec((None, None, psz, width),
                                  lambda b, j, pt_ref: (l, pt_ref[b * n_pages + j * ppb + p], 0, 0))
    s_all, gate = pl.pallas_call(
        functools.partial(_cache_logits_kernel, pages=ppb),
        grid_spec=pltpu.PrefetchScalarGridSpec(
            num_scalar_prefetch=1, grid=(n_req, nblk),
            in_specs=[page(p) for p in range(ppb)] + [pl.BlockSpec((None, width, lanes), lambda b, j, pt_ref: (b, 0, 0))],
            out_specs=[pl.BlockSpec((None, blk, lanes), lambda b, j, pt_ref: (b, j, 0)),
                       pl.BlockSpec((None, None, 1, lanes), lambda b, j, pt_ref: (b, j, 0, 0))]),
        out_shape=[jax.ShapeDtypeStruct((n_req, past, lanes), F32),
                   jax.ShapeDtypeStruct((n_req, nblk, 1, lanes), F32)],
        compiler_params=_params(("parallel", "arbitrary")), name="cache_logits",
    )(pt, *([ck] * ppb), qbd)

    tq = np.arange(lanes) % 8
    hq = np.arange(lanes) // 8
    srow = np.arange(blk)[:, None]
    b_last = rel_bias[_t5_bucket_np(blk + tq[None, :] - srow), hq[None, :]]
    assert (_t5_bucket_np(np.array([blk + 1, past + 8])) == N_BUCKETS - 1).all()
    far = rel_bias[N_BUCKETS - 1][hq][None, :]
    orow = np.arange(rows)[:, None]
    b_own = rel_bias[_t5_bucket_np(tq[None, :] - orow), hq[None, :]]
    m_own = jnp.asarray(((orow <= tq[None, :]) & (orow < t_valid)).astype(np.float32))

    kcol = col0 + width
    vcol = col0 + 2 * width
    full2 = lambda b: (0, 0)
    probs, p_own = pl.pallas_call(
        functools.partial(_cache_select_kernel, nblk=nblk),
        grid=(n_req,),
        in_specs=[pl.BlockSpec((None, past, lanes), lambda b: (b, 0, 0)),
                  pl.BlockSpec((None, nblk, lanes), lambda b: (b, 0, 0)),
                  pl.BlockSpec((rows, width), lambda b: (b, kcol // width)),
                  pl.BlockSpec((None, width, lanes), lambda b: (b, 0, 0)),
                  pl.BlockSpec((blk, lanes), full2), pl.BlockSpec((1, lanes), full2),
                  pl.BlockSpec((rows, lanes), full2), pl.BlockSpec((rows, lanes), full2)],
        out_specs=[pl.BlockSpec((None, past, lanes), lambda b: (b, 0, 0)),
                   pl.BlockSpec((None, rows, lanes), lambda b: (b, 0, 0))],
        out_shape=[jax.ShapeDtypeStruct((n_req, past, lanes), F32),
                   jax.ShapeDtypeStruct((n_req, rows, lanes), F32)],
        compiler_params=_params(("parallel",)), name="cache_select",
    )(s_all, gate.reshape(n_req, nblk, lanes), proj_s, qbd, b_last, far, b_own, m_own)

    out_full = pl.pallas_call(
        functools.partial(_cache_pv_kernel, pages=ppb),
        grid_spec=pltpu.PrefetchScalarGridSpec(
            num_scalar_prefetch=1, grid=(n_req, nblk),
            in_specs=[page(p) for p in range(ppb)]
            + [pl.BlockSpec((None, blk, lanes), lambda b, j, pt_ref: (b, j, 0)),
               pl.BlockSpec((None, rows, lanes), lambda b, j, pt_ref: (b, 0, 0)),
               pl.BlockSpec((rows, width), lambda b, j, pt_ref: (b, vcol // width))],
            out_specs=pl.BlockSpec((None, lanes, width), lambda b, j, pt_ref: (b, 0, 0)),
            scratch_shapes=[pltpu.VMEM((lanes, width), F32)]),
        out_shape=jax.ShapeDtypeStruct((n_req, lanes, width), F32),
        compiler_params=_params(("parallel", "arbitrary")), name="cache_pv",
    )(pt, *([cv] * ppb), probs, p_own, proj_s)

    o = out_full.reshape(n_req, heads, 8, heads, GROUP_DIM)
    o = jnp.stack([o[:, h, :, h, :] for h in range(heads)], axis=2)
    o = jnp.pad(o, ((0, 0), (0, rows - 8), (0, 0), (0, 0)))
    return o.reshape(n_req * rows, width).astype(BF16)


def _merge_kernel(oa_ref, ob_ref, oc_ref, wa_ref, wb_ref, wc_ref, ga_ref, gb_ref, gc_ref, o_ref):
    def branch(o_r, w_r, g_r):
        y = jnp.dot(o_r[...], w_r[...].astype(BF16), preferred_element_type=F32)
        return _sigmoid(g_r[...]) * y
    o_ref[...] = (branch(oa_ref, wa_ref, ga_ref) + branch(ob_ref, wb_ref, gb_ref)
                  + branch(oc_ref, wc_ref, gc_ref)).astype(o_ref.dtype)


def _merge(o_a, o_b, o_c, w_a, w_b, w_c, l, proj, gcol0, *, tm=512, tn=512):
    m = o_a.shape[0]
    d = w_a.shape[2]
    tm = min(tm, m)
    assert gcol0 % tn == 0 and d % tn == 0
    gbase = gcol0 // tn
    nd = d // tn
    act = lambda o: pl.BlockSpec((tm, o.shape[1]), lambda i, j: (i, 0))
    wsp = lambda w: pl.BlockSpec((None, w.shape[1], tn), lambda i, j: (l, 0, j))
    gsp = lambda br: pl.BlockSpec((tm, tn), lambda i, j: (i, gbase + br * nd + j))
    return pl.pallas_call(
        _merge_kernel, grid=(m // tm, d // tn),
        in_specs=[act(o_a), act(o_b), act(o_c), wsp(w_a), wsp(w_b), wsp(w_c), gsp(0), gsp(1), gsp(2)],
        out_specs=pl.BlockSpec((tm, tn), lambda i, j: (i, j)),
        out_shape=jax.ShapeDtypeStruct((m, d), BF16),
        compiler_params=_params(("parallel", "parallel")), name="merge",
    )(o_a, o_b, o_c, w_a, w_b, w_c, proj, proj, proj)


def _staircase():
    return [(r1, r2) for r1 in range(PEER_TOPK) for r2 in range(PEER_TOPK) if (r1 + 1) * (r2 + 1) <= PEER_TOPK]


def _top_values(s, count):
    rid = lax.broadcasted_iota(jnp.int32, s.shape, 0)
    big = s.shape[0]
    out = []
    for _ in range(count):
        mx = jnp.max(s, axis=0, keepdims=True)
        out.append(mx)
        idx = jnp.min(jnp.where(s == mx, rid, big), axis=0, keepdims=True)
        s = jnp.where(rid == idx, -jnp.inf, s)
    return out


def _peer_retrieve_kernel(q_ref, k1_ref, k2_ref, s1_ref, s2_ref, c1_ref, e2_ref, tau_ref, *, heads, half):
    nt = (((1,), (1,)), ((), ()))
    pairs = _staircase()
    for h in range(heads):
        q1 = q_ref[:, (2 * h) * half:(2 * h + 1) * half]
        q2 = q_ref[:, (2 * h + 1) * half:(2 * h + 2) * half]
        s1 = lax.dot_general(k1_ref[h], q1, nt, preferred_element_type=F32)
        s2 = lax.dot_general(k2_ref[h], q2, nt, preferred_element_type=F32)
        v1 = _top_values(s1, PEER_TOPK)
        v2 = _top_values(s2, PEER_TOPK)
        fill = [jnp.full(v1[0].shape, -jnp.inf, F32)] * ((-len(pairs)) % 8)
        cand = jnp.concatenate([v1[a] + v2[b] for a, b in pairs] + fill, axis=0)
        tau = _top_values(cand, PEER_TOPK)[-1]
        e1 = jnp.exp(s1 - v1[0])
        e2 = jnp.exp(s2 - v2[0])
        z = jnp.sum(jnp.where(cand >= tau, jnp.exp(cand - (v1[0] + v2[0])), 0.0), axis=0, keepdims=True)
        s1_ref[h] = s1
        s2_ref[h] = s2
        c1_ref[h] = e1 / z
        e2_ref[h] = e2
        tau_ref[h] = tau


def _peer_retrieve(q, keys1, keys2, *, tm=256):
    m = q.shape[0]
    heads, nkeys, half = keys1.shape
    tm = min(tm, m)
    tok = pl.BlockSpec((heads, nkeys, tm), lambda i: (0, 0, i))
    keyspec = pl.BlockSpec((heads, nkeys, half), lambda i: (0, 0, 0))
    big = jax.ShapeDtypeStruct((heads, nkeys, m), F32)
    return pl.pallas_call(
        functools.partial(_peer_retrieve_kernel, heads=heads, half=half),
        grid=(m // tm,),
        in_specs=[pl.BlockSpec((tm, q.shape[1]), lambda i: (i, 0)), keyspec, keyspec],
        out_specs=[tok, tok, tok, tok, pl.BlockSpec((heads, 1, tm), lambda i: (0, 0, i))],
        out_shape=[big, big, big, big, jax.ShapeDtypeStruct((heads, 1, m), F32)],
        compiler_params=_params(("parallel",)), name="peer_retrieve",
    )(q, keys1, keys2)


def _peer_dense_kernel(h_ref, u_ref, v_ref, s1_ref, c1_ref, s2_ref, e2_ref, tau_ref, y_ref, *, heads, nsub, nkeys):
    j = pl.program_id(1)
    at = lax.dot_general(u_ref[...], h_ref[...], (((1,), (1,)), ((), ())), preferred_element_type=F32)
    act = _gelu(at)
    parts = []
    for a in range(nsub):
        w = None
        for h in range(heads):
            hit = (s1_ref[h, a:a + 1, :] + s2_ref[h]) >= tau_ref[h]
            term = jnp.where(hit, e2_ref[h], 0.0) * c1_ref[h, a:a + 1, :]
            w = term if w is None else w + term
        parts.append((w * act[a * nkeys:(a + 1) * nkeys, :]).astype(BF16))
    pt = jnp.concatenate(parts, axis=0) if nsub > 1 else parts[0]
    y = lax.dot_general(pt, v_ref[...], (((0,), (0,)), ((), ())), preferred_element_type=F32)

    @pl.when(j == 0)
    def _():
        y_ref[...] = y

    @pl.when(j > 0)
    def _():
        y_ref[...] += y


def _peer_dense(h, u_tab, v_tab, l, s1, s2, c1, e2, tau, *, tm=512, tn=256):
    m, d = h.shape
    n_exp = u_tab.shape[1]
    heads, nkeys, _ = s1.shape
    tm = min(tm, m)
    nsub = tn // nkeys
    s1r = s1.reshape(heads, nkeys // nsub, nsub, m)
    c1r = c1.reshape(heads, nkeys // nsub, nsub, m)
    sub = pl.BlockSpec((heads, None, nsub, tm), lambda i, j: (0, j, 0, i))
    tok = pl.BlockSpec((heads, nkeys, tm), lambda i, j: (0, 0, i))
    return pl.pallas_call(
        functools.partial(_peer_dense_kernel, heads=heads, nsub=nsub, nkeys=nkeys),
        grid=(m // tm, n_exp // tn),
        in_specs=[pl.BlockSpec((tm, d), lambda i, j: (i, 0)),
                  pl.BlockSpec((None, tn, d), lambda i, j: (l, j, 0)),
                  pl.BlockSpec((None, tn, d), lambda i, j: (l, j, 0)),
                  sub, sub, tok, tok, pl.BlockSpec((heads, 1, tm), lambda i, j: (0, 0, i))],
        out_specs=pl.BlockSpec((tm, d), lambda i, j: (i, 0)),
        out_shape=jax.ShapeDtypeStruct((m, d), F32),
        compiler_params=_params(("parallel", "arbitrary")), name="peer_dense",
    )(h, u_tab, v_tab, s1r, c1r, s2, e2, tau)


def _split_mod(mod, d, per_row):
    parts = [mod[:, k * d:(k + 1) * d] for k in range(6)]
    if per_row is None:
        return [p[:, None, :] for p in parts]
    return [jnp.repeat(p, per_row, axis=0)[None] for p in parts]


def _spatial_weights(w_s, b_s, *, t_valid, width):
    groups, chunk, _ = w_s.shape
    if t_valid is None:
        w = jnp.where(np.tril(np.ones((chunk, chunk), bool))[None], w_s, 0.0)
        b = b_s.T
    else:
        n_rep = chunk // SAMPLE_ROWS
        small = jnp.where(np.tril(np.ones((t_valid, t_valid), bool))[None], w_s[:, :t_valid, :t_valid], 0.0)
        small = jnp.pad(small, ((0, 0), (0, SAMPLE_ROWS - t_valid), (0, SAMPLE_ROWS - t_valid)))
        w = jnp.einsum("rs,gab->grasb", jnp.eye(n_rep, dtype=F32), small).reshape(groups, chunk, chunk)
        b = jnp.tile(jnp.pad(b_s[:, :t_valid].T, ((0, SAMPLE_ROWS - t_valid), (0, 0))), (n_rep, 1))
    return w, jnp.repeat(b, width // groups, axis=1)


def _layer(p, l, lb, x, pending, mod, s0, attn_fn, *, n_req, t_len, t_valid, rows_per_group, hgrn_chunk, dims):
    d, a_w, b_w, c_w = dims
    sh1, sc1, g1, sh2, sc2, g2 = mod
    y_prev, gate_prev = pending if pending is not None else (None, None)
    x_new, h = _rownorm(x, p["norm_mix"][l], sc=sc1, sh=sh1, y=y_prev, gate=gate_prev,
                        rows_per_group=rows_per_group, out_dtype=BF16)
    x = x if pending is None else x_new
    proj = _matmul(h, p["w_in"], l)
    col_b = 2 * a_w
    col_c = col_b + 4 * b_w
    col_g = col_c + 3 * c_w
    w_eff, b_eff = _spatial_weights(p["w_spatial"][l], p["b_spatial"][l], t_valid=t_valid, width=a_w)
    res_a = _branch_a(proj, w_eff, b_eff, p["ln_v_gain"][l], p["ln_v_bias"][l], width=a_w,
                      want_av=t_valid is not None)
    o_a = res_a[0]
    a_v = res_a[1] if t_valid is not None else None
    o_b, s_fin = _hgrn(proj, col_b, lb, s0, p["hgrn_norm"][l], n_req=n_req, t_len=t_len, t_valid=t_valid,
                       chunk=hgrn_chunk)
    o_c = attn_fn(proj, col_c)
    merged = _merge(o_a, o_b, o_c, p["w_branch_a"], p["w_branch_b"], p["w_branch_c"], l, proj, col_g)
    x1 = _matmul(merged, p["w_out"], l, resid=x, gate=g1, rows_per_group=rows_per_group)
    _, h2 = _rownorm(x1, p["norm_ffn"][l], sc=sc2, sh=sh2, rows_per_group=rows_per_group, out_dtype=BF16)
    q = _matmul(h2, p["peer_wq"], l)
    s1, s2, c1, e2, tau = _peer_retrieve(q, p["peer_keys1"][l], p["peer_keys2"][l])
    y = _peer_dense(h2, p["peer_u16"], p["peer_v16"], l, s1, s2, c1, e2, tau)
    k_new = proj[:, col_c + c_w:col_c + 2 * c_w]
    v_new = proj[:, col_c + 2 * c_w:col_c + 3 * c_w]
    return x1, (y, g2), a_v, k_new, v_new, s_fin


def kernel(x_prompt, x_sample, cache_k, cache_v, page_table, state_hgrn, c_prompt, c_sample, w_ada, b_ada, norm_mix, norm_ffn, norm_final, w_in, ln_v_gain, ln_v_bias, w_spatial, b_spatial, hgrn_gamma, hgrn_norm, rel_bias, w_branch_a, w_branch_b, w_branch_c, w_out, peer_wq, peer_keys1, peer_keys2, peer_u, peer_v):
    depth = w_in.shape[0]
    n_p, t_p, d = x_prompt.shape
    n_s, t_s, _ = x_sample.shape
    a_w = ln_v_gain.shape[1]
    b_w = hgrn_gamma.shape[1]
    heads_b = b_w // GROUP_DIM
    heads_c = cache_k.shape[3]
    c_w = heads_c * GROUP_DIM
    dims = (d, a_w, b_w, c_w)
    rows = SAMPLE_ROWS

    p = dict(norm_mix=norm_mix, norm_ffn=norm_ffn, w_in=w_in, ln_v_gain=ln_v_gain, ln_v_bias=ln_v_bias,
             w_spatial=w_spatial, b_spatial=b_spatial, hgrn_norm=hgrn_norm, w_branch_a=w_branch_a,
             w_branch_b=w_branch_b, w_branch_c=w_branch_c, w_out=w_out, peer_wq=peer_wq,
             peer_keys1=peer_keys1, peer_keys2=peer_keys2,
             peer_u16=peer_u.astype(BF16), peer_v16=peer_v.astype(BF16))

    gam = jax.nn.softmax(hgrn_gamma.astype(F32), axis=0)
    lower_bounds = (jnp.cumsum(gam, axis=0) - gam[:1]).reshape(depth, heads_b, GROUP_DIM)

    xp = x_prompt.reshape(n_p * t_p, d)
    xs = jnp.pad(x_sample, ((0, 0), (0, rows - t_s), (0, 0))).reshape(n_s * rows, d)
    c_all = jnp.pad(jnp.concatenate([c_prompt, c_sample], axis=0), ((0, (-(n_p + n_s)) % 8), (0, 0)))
    s0_prompt = jnp.zeros((n_p, heads_b, GROUP_DIM, GROUP_DIM), F32)
    bias_tiles = _moba_bias_tiles(rel_bias)

    pend_p = pend_s = None
    outs = {k: [] for k in ("kp", "vp", "ks", "vs", "hp", "hs", "av")}
    for l in range(depth):
        mod = _matmul(c_all, w_ada, l, bias=b_ada, pre_silu=True, tn=1024)
        mod_p = _split_mod(mod[:n_p], d, None)
        mod_s = _split_mod(mod[n_p:n_p + n_s], d, rows)
        attn_p = lambda proj, col: _moba_prompt(proj, col, bias_tiles, n_req=n_p, t_len=t_p, heads=heads_c)
        xp, pend_p, _, k_p, v_p, s_p = _layer(
            p, l, lower_bounds[l], xp, pend_p, mod_p, s0_prompt, attn_p, n_req=n_p, t_len=t_p, t_valid=None,
            rows_per_group=t_p, hgrn_chunk=min(256, t_p), dims=dims)
        attn_s = lambda proj, col: _moba_sample(proj, col, cache_k, cache_v, l, page_table, rel_bias,
                                                heads=heads_c, t_valid=t_s)
        xs, pend_s, av_s, k_s, v_s, s_s = _layer(
            p, l, lower_bounds[l], xs, pend_s, mod_s, state_hgrn[l], attn_s, n_req=n_s, t_len=rows, t_valid=t_s,
            rows_per_group=None, hgrn_chunk=rows, dims=dims)
        outs["kp"].append(k_p.reshape(n_p, t_p, heads_c, GROUP_DIM))
        outs["vp"].append(v_p.reshape(n_p, t_p, heads_c, GROUP_DIM))
        outs["ks"].append(k_s.reshape(n_s, rows, heads_c, GROUP_DIM)[:, :t_s])
        outs["vs"].append(v_s.reshape(n_s, rows, heads_c, GROUP_DIM)[:, :t_s])
        outs["hp"].append(s_p)
        outs["hs"].append(s_s)
        outs["av"].append(av_s.reshape(n_s, rows, a_w)[:, :t_s])

    _, y_prompt = _rownorm(xp, norm_final, y=pend_p[0], gate=pend_p[1], rows_per_group=t_p, out_dtype=F32)
    _, y_sample = _rownorm(xs, norm_final, y=pend_s[0], gate=pend_s[1], rows_per_group=None, out_dtype=F32)
    y_prompt = y_prompt.reshape(n_p, t_p, d)
    y_sample = y_sample.reshape(n_s, rows, d)[:, :t_s]
    st = jnp.stack
    return (y_prompt, y_sample, st(outs["kp"]), st(outs["vp"]), st(outs["ks"]), st(outs["vs"]),
            st(outs["hp"]), st(outs["hs"]), st(outs["av"]))
```

```python
import functools
import math

import numpy as np
import jax
import jax.numpy as jnp
from jax import lax
from jax.experimental import pallas as pl
from jax.experimental.pallas import tpu as pltpu

F32 = jnp.float32
BF16 = jnp.bfloat16

EPS = 1e-6
NEG_BIG = -1e30
LB_FLOOR = 1e-30

GROUP_DIM = 128
MOBA_BLOCK = 256
MOBA_TOPK = 3
N_BUCKETS = 32
MAX_DISTANCE = 128
PEER_TOPK = 16
HGRN_SUB = 16
SAMPLE_ROWS = 16

VMEM_LIMIT_V7X = 56 * 1024 * 1024


def _params(semantics):
    return pltpu.CompilerParams(dimension_semantics=semantics, vmem_limit_bytes=VMEM_LIMIT_V7X)


def _gelu(x):
    return jax.nn.gelu(x)


def _sigmoid(x):
    return jax.nn.sigmoid(x)


def _group_spec(arr, tm, rows_per_group, ncol_block, col_of):
    g, r, _ = arr.shape
    if r == 1:
        return pl.BlockSpec((None, 1, ncol_block), lambda *ix: ((ix[0] * tm) // rows_per_group, 0, col_of(*ix)))
    assert g == 1
    return pl.BlockSpec((None, tm, ncol_block), lambda *ix: (0, ix[0], col_of(*ix)))


def _rownorm_kernel(*refs, has_res, has_mod):
    it = iter(refs)
    x_ref = next(it)
    if has_res:
        y_ref = next(it)
        g_ref = next(it)
    gain_ref = next(it)
    if has_mod:
        sc_ref = next(it)
        sh_ref = next(it)
    if has_res:
        xo_ref = next(it)
    h_ref = next(it)
    x = x_ref[...]
    if has_res:
        x = x + g_ref[...] * y_ref[...]
        xo_ref[...] = x
    h = x * lax.rsqrt(jnp.mean(x * x, axis=-1, keepdims=True) + EPS) * gain_ref[...]
    if has_mod:
        h = h * (1.0 + sc_ref[...]) + sh_ref[...]
    h_ref[...] = h.astype(h_ref.dtype)


def _rownorm(x, gain, *, sc=None, sh=None, y=None, gate=None, rows_per_group, out_dtype, tm=256):
    m, d = x.shape
    tm = min(tm, m)
    has_res, has_mod = y is not None, sc is not None
    row = pl.BlockSpec((tm, d), lambda i: (i, 0))
    zero = lambda i: 0
    ins, specs = [x], [row]
    if has_res:
        ins += [y, gate]
        specs += [row, _group_spec(gate, tm, rows_per_group, d, zero)]
    ins.append(gain.reshape(1, d))
    specs.append(pl.BlockSpec((1, d), lambda i: (0, 0)))
    if has_mod:
        ins += [sc, sh]
        specs += [_group_spec(sc, tm, rows_per_group, d, zero), _group_spec(sh, tm, rows_per_group, d, zero)]
    out_shape, out_specs = [], []
    if has_res:
        out_shape.append(jax.ShapeDtypeStruct((m, d), F32))
        out_specs.append(row)
    out_shape.append(jax.ShapeDtypeStruct((m, d), out_dtype))
    out_specs.append(row)
    res = pl.pallas_call(
        functools.partial(_rownorm_kernel, has_res=has_res, has_mod=has_mod),
        grid=(m // tm,), in_specs=specs, out_specs=out_specs, out_shape=out_shape,
        compiler_params=_params(("parallel",)), name="rownorm",
    )(*ins)
    return (res[0], res[1]) if has_res else (None, res[0])


def _mm_kernel(*refs, nk, pre_silu, has_bias, has_res):
    it = iter(refs)
    x_ref = next(it)
    w_ref = next(it)
    b_ref = next(it) if has_bias else None
    if has_res:
        r_ref = next(it)
        g_ref = next(it)
    o_ref = next(it)
    acc_ref = next(it) if nk > 1 else None
    x = x_ref[...]
    if pre_silu:
        x = x * _sigmoid(x)
    part = jnp.dot(x.astype(BF16), w_ref[...].astype(BF16), preferred_element_type=F32)

    def finish(acc):
        if has_bias:
            acc = acc + b_ref[...]
        if has_res:
            acc = r_ref[...] + g_ref[...] * acc
        o_ref[...] = acc.astype(o_ref.dtype)

    if nk == 1:
        finish(part)
    else:
        k = pl.program_id(2)

        @pl.when(k == 0)
        def _():
            acc_ref[...] = part

        @pl.when(k > 0)
        def _():
            acc_ref[...] += part

        @pl.when(k == nk - 1)
        def _():
            finish(acc_ref[...])


def _matmul(x, w, l, *, bias=None, resid=None, gate=None, rows_per_group=None, pre_silu=False,
            out_dtype=F32, tm=1024, tn=512, tk=None):
    m, kdim = x.shape
    n = w.shape[2]
    tm, tn = min(tm, m), min(tn, n)
    tk = kdim if tk is None else min(tk, kdim)
    nk = kdim // tk
    has_bias, has_res = bias is not None, resid is not None
    ins = [x, w]
    specs = [pl.BlockSpec((tm, tk), lambda i, j, k: (i, k)),
             pl.BlockSpec((None, tk, tn), lambda i, j, k: (l, k, j))]
    if has_bias:
        ins.append(bias.reshape(bias.shape[0], 1, n))
        specs.append(pl.BlockSpec((None, 1, tn), lambda i, j, k: (l, 0, j)))
    if has_res:
        ins += [resid, gate]
        specs += [pl.BlockSpec((tm, tn), lambda i, j, k: (i, j)),
                  _group_spec(gate, tm, rows_per_group, tn, lambda i, j, k: j)]
    return pl.pallas_call(
        functools.partial(_mm_kernel, nk=nk, pre_silu=pre_silu, has_bias=has_bias, has_res=has_res),
        grid=(m // tm, n // tn, nk), in_specs=specs,
        out_specs=pl.BlockSpec((tm, tn), lambda i, j, k: (i, j)),
        out_shape=jax.ShapeDtypeStruct((m, n), out_dtype),
        scratch_shapes=[pltpu.VMEM((tm, tn), F32)] if nk > 1 else [],
        compiler_params=_params(("parallel", "parallel", "arbitrary")), name="matmul",
    )(*ins)


def _branch_a_kernel(u_ref, v_ref, w_ref, b_ref, lg_ref, lb_ref, o_ref, *av_ref, groups):
    u = _gelu(u_ref[...])
    v = _gelu(v_ref[...])
    mu = jnp.mean(v, axis=-1, keepdims=True)
    vc = v - mu
    var = jnp.mean(vc * vc, axis=-1, keepdims=True)
    v = vc * lax.rsqrt(var + EPS) * lg_ref[...] + lb_ref[...]
    if av_ref:
        av_ref[0][...] = v
    vb = v.astype(BF16)
    for g in range(groups):
        cols = slice(g * GROUP_DIM, (g + 1) * GROUP_DIM)
        mixed = jnp.dot(w_ref[g].astype(BF16), vb[:, cols], preferred_element_type=F32) + b_ref[:, cols]
        o_ref[:, cols] = (u[:, cols] * mixed).astype(o_ref.dtype)


def _branch_a(proj, w_eff, b_eff, ln_gain, ln_bias, *, width, want_av):
    m = proj.shape[0]
    groups, chunk, _ = w_eff.shape
    full = lambda i: (0, 0)
    out_shape = [jax.ShapeDtypeStruct((m, width), BF16)]
    out_specs = [pl.BlockSpec((chunk, width), lambda i: (i, 0))]
    if want_av:
        out_shape.append(jax.ShapeDtypeStruct((m, width), F32))
        out_specs.append(pl.BlockSpec((chunk, width), lambda i: (i, 0)))
    return pl.pallas_call(
        functools.partial(_branch_a_kernel, groups=groups),
        grid=(m // chunk,),
        in_specs=[pl.BlockSpec((chunk, width), lambda i: (i, 0)), pl.BlockSpec((chunk, width), lambda i: (i, 1)),
                  pl.BlockSpec((groups, chunk, chunk), lambda i: (0, 0, 0)), pl.BlockSpec((chunk, width), full),
                  pl.BlockSpec((1, width), full), pl.BlockSpec((1, width), full)],
        out_specs=out_specs, out_shape=out_shape,
        compiler_params=_params(("parallel",)), name="branch_a",
    )(proj, proj, w_eff, b_eff, ln_gain.reshape(1, width), ln_bias.reshape(1, width))


def _hgrn_kernel(q_ref, f_ref, i_ref, g_ref, lb_ref, s0_ref, gain_ref, o_ref, sfin_ref,
                 st_s, q_s, k_s, b_s, o_s, *, chunk, t_valid):
    c = pl.program_id(2)

    @pl.when(c == 0)
    def _():
        st_s[...] = s0_ref[...].T

    z = f_ref[...]
    lb = lb_ref[...]
    soft = jnp.log1p(jnp.exp(-jnp.abs(z)))
    ls_pos = jnp.minimum(z, 0.0) - soft
    ls_neg = jnp.minimum(-z, 0.0) - soft
    t2 = jnp.log(jnp.maximum(lb, LB_FLOOR)) + ls_neg
    log_f = jnp.maximum(ls_pos, t2) + jnp.log1p(jnp.exp(-jnp.abs(ls_pos - t2)))
    k_in = (1.0 - lb) * _sigmoid(-z)
    row = lax.broadcasted_iota(jnp.int32, z.shape, 0)
    sub = row % HGRN_SUB
    if t_valid is not None:
        live = row < t_valid
        log_f = jnp.where(live, log_f, 0.0)
        k_in = jnp.where(live, k_in, 0.0)
    b = log_f
    step = 1
    while step < HGRN_SUB:
        b = b + jnp.where(sub >= step, pltpu.roll(b, step, 0), 0.0)
        step *= 2
    qv = q_ref[...]
    q_s[...] = qv * _sigmoid(qv)
    k_s[...] = k_in
    b_s[...] = b

    ti = lax.broadcasted_iota(jnp.int32, (HGRN_SUB, HGRN_SUB, GROUP_DIM), 0)
    si = lax.broadcasted_iota(jnp.int32, (HGRN_SUB, HGRN_SUB, GROUP_DIM), 1)
    causal = si <= ti
    ones = jnp.ones((GROUP_DIM, GROUP_DIM), BF16)

    st = st_s[...]
    for j in range(chunk // HGRN_SUB):
        r = slice(j * HGRN_SUB, (j + 1) * HGRN_SUB)
        qb, kb, bb, ib = q_s[r, :], k_s[r, :], b_s[r, :], i_ref[r, :]
        bl = bb[HGRN_SUB - 1:HGRN_SUB, :]
        inter = lax.dot_general(qb * jnp.exp(bb), st, (((1,), (1,)), ((), ())), preferred_element_type=F32)
        dec = jnp.exp(jnp.where(causal, bb[:, None, :] - bb[None, :, :], NEG_BIG))
        a = (dec * qb[:, None, :]) * kb[None, :, :]
        rs = jnp.dot(a.reshape(HGRN_SUB * HGRN_SUB, GROUP_DIM).astype(BF16), ones,
                     preferred_element_type=F32)
        intra = jnp.sum(rs.reshape(HGRN_SUB, HGRN_SUB, GROUP_DIM) * ib[None, :, :], axis=1)
        o_s[r, :] = inter + intra
        ke = kb * jnp.exp(bl - bb)
        upd = lax.dot_general(ib, ke, (((0,), (0,)), ((), ())), preferred_element_type=F32)
        st = st * jnp.exp(bl) + upd
    st_s[...] = st

    o = o_s[...]
    o = o * lax.rsqrt(jnp.mean(o * o, axis=-1, keepdims=True) + EPS) * gain_ref[...]
    gv = g_ref[...]
    o_ref[...] = (o * (gv * _sigmoid(gv))).astype(o_ref.dtype)

    @pl.when(c == pl.num_programs(2) - 1)
    def _():
        sfin_ref[...] = st_s[...].T


def _hgrn(proj, col0, lb, s0, norm_gain, *, n_req, t_len, t_valid, chunk):
    heads = lb.shape[0]
    width = heads * GROUP_DIM
    base = col0 // GROUP_DIM
    nch = t_len // chunk
    col = lambda off: pl.BlockSpec((chunk, GROUP_DIM), lambda b, h, c: (b * nch + c, base + off * heads + h))
    o, s_fin = pl.pallas_call(
        functools.partial(_hgrn_kernel, chunk=chunk, t_valid=t_valid),
        grid=(n_req, heads, nch),
        in_specs=[col(0), col(1), col(2), col(3),
                  pl.BlockSpec((None, 1, GROUP_DIM), lambda b, h, c: (h, 0, 0)),
                  pl.BlockSpec((None, None, GROUP_DIM, GROUP_DIM), lambda b, h, c: (b, h, 0, 0)),
                  pl.BlockSpec((1, GROUP_DIM), lambda b, h, c: (0, 0))],
        out_specs=[pl.BlockSpec((chunk, GROUP_DIM), lambda b, h, c: (b * nch + c, h)),
                   pl.BlockSpec((None, None, GROUP_DIM, GROUP_DIM), lambda b, h, c: (b, h, 0, 0))],
        out_shape=[jax.ShapeDtypeStruct((n_req * t_len, width), BF16),
                   jax.ShapeDtypeStruct((n_req, heads, GROUP_DIM, GROUP_DIM), F32)],
        scratch_shapes=[pltpu.VMEM((GROUP_DIM, GROUP_DIM), F32)] + [pltpu.VMEM((chunk, GROUP_DIM), F32)] * 4,
        compiler_params=_params(("parallel", "parallel", "arbitrary")), name="hgrn",
    )(proj, proj, proj, proj, lb.reshape(heads, 1, GROUP_DIM), s0, norm_gain.reshape(1, GROUP_DIM))
    return o, s_fin


def _t5_bucket_np(rel):
    n = np.maximum(rel, 0)
    max_exact = N_BUCKETS // 2
    nf = np.maximum(n, max_exact).astype(np.float32)
    large = max_exact + (np.log(nf / np.float32(max_exact)) / np.float32(math.log(MAX_DISTANCE / max_exact))
                         * np.float32(N_BUCKETS - max_exact)).astype(np.int32)
    large = np.minimum(large, N_BUCKETS - 1)
    return np.where(n < max_exact, n, large).astype(np.int32)


def _top_blocks(gate, n_past, n_keep):
    nblk = gate.shape[0]
    rid = lax.broadcasted_iota(jnp.int32, gate.shape, 0)
    g = jnp.where(rid < n_past, gate, NEG_BIG)
    sel = jnp.zeros(gate.shape, F32)
    for r in range(min(MOBA_TOPK, nblk)):
        mx = jnp.max(g, axis=0, keepdims=True)
        idx = jnp.min(jnp.where(g == mx, rid, nblk), axis=0, keepdims=True)
        pick = rid == idx
        live = jnp.logical_and(pick, jnp.full(gate.shape, r, jnp.int32) < n_keep)
        sel = jnp.where(live, 1.0, sel)
        g = jnp.where(pick, -jnp.inf, g)
    return sel


def _moba_prompt_kernel(q_ref, k_ref, v_ref, bo_ref, bp_ref, far_ref, o_ref, *, nblk):
    blk = MOBA_BLOCK
    scale = GROUP_DIM ** -0.5
    nt = (((1,), (1,)), ((), ()))
    kb = k_ref[...].astype(BF16)
    vb = v_ref[...].astype(BF16)
    k_mean = jnp.concatenate(
        [jnp.mean(k_ref[j * blk:(j + 1) * blk, :], axis=0, keepdims=True) for j in range(nblk)], axis=0)
    ks = lax.broadcasted_iota(jnp.int32, (blk, blk), 0)
    qs = lax.broadcasted_iota(jnp.int32, (blk, blk), 1)
    causal = ks <= qs
    for i in range(nblk):
        rows = slice(i * blk, (i + 1) * blk)
        n = (i + 1) * blk
        qf = q_ref[rows, :]
        s = lax.dot_general(kb[:n], qf.astype(BF16), nt, preferred_element_type=F32) * scale
        if i > 0:
            gate = lax.dot_general(k_mean, qf, nt, preferred_element_type=F32)
            sel = _top_blocks(gate, i, i)
        parts = []
        for j in range(i + 1):
            sj = s[j * blk:(j + 1) * blk]
            if j == i:
                sj = jnp.where(causal, sj + bo_ref[...], NEG_BIG)
            else:
                bias = bp_ref[...] if j == i - 1 else far_ref[...]
                sj = jnp.where(sel[j:j + 1, :] > 0.0, sj + bias, NEG_BIG)
            parts.append(sj)
        mx = jnp.max(parts[0], axis=0, keepdims=True)
        for sj in parts[1:]:
            mx = jnp.maximum(mx, jnp.max(sj, axis=0, keepdims=True))
        probs = [jnp.exp(sj - mx) for sj in parts]
        den = jnp.sum(probs[0], axis=0, keepdims=True)
        for pj in probs[1:]:
            den = den + jnp.sum(pj, axis=0, keepdims=True)
        p = (jnp.concatenate(probs, axis=0) if i > 0 else probs[0]).astype(BF16)
        acc = lax.dot_general(vb[:n], p, (((0,), (0,)), ((), ())), preferred_element_type=F32)
        o_ref[rows, :] = (acc / den).T.astype(o_ref.dtype)


def _moba_bias_tiles(rel_bias):
    blk = MOBA_BLOCK
    s = np.arange(blk)[:, None]
    t = np.arange(blk)[None, :]
    own = jnp.transpose(_bias_lookup(rel_bias, _t5_bucket_np(t - s)), (2, 0, 1))
    prev = jnp.transpose(_bias_lookup(rel_bias, _t5_bucket_np(blk + t - s)), (2, 0, 1))
    far_bucket = _t5_bucket_np(np.array([blk + 1, 2 * blk, 1 << 20]))
    assert far_bucket.min() == far_bucket.max() == N_BUCKETS - 1
    far = jnp.broadcast_to(rel_bias[N_BUCKETS - 1][:, None, None], (rel_bias.shape[1], 1, blk))
    return own, prev, far


def _bias_lookup(rel_bias, buckets, heads=None):
    hot = (jnp.asarray(buckets)[..., None] == jnp.arange(N_BUCKETS)).astype(F32)
    vals = jnp.einsum("...n,nh->...h", hot, rel_bias, precision=lax.Precision.HIGHEST)
    if heads is None:
        return vals
    pick = (jnp.asarray(heads)[:, None] == jnp.arange(rel_bias.shape[1])).astype(F32)
    return jnp.einsum("...lh,lh->...l", vals, pick, precision=lax.Precision.HIGHEST)


def _moba_prompt(proj, col0, bias_tiles, *, n_req, t_len, heads):
    blk = MOBA_BLOCK
    nblk = t_len // blk
    base = col0 // GROUP_DIM
    own, prev, far = bias_tiles
    seq = lambda off: pl.BlockSpec((t_len, GROUP_DIM), lambda b, h: (b, base + off * heads + h))
    tile = pl.BlockSpec((None, blk, blk), lambda b, h: (h, 0, 0))
    return pl.pallas_call(
        functools.partial(_moba_prompt_kernel, nblk=nblk),
        grid=(n_req, heads),
        in_specs=[seq(0), seq(1), seq(2), tile, tile, pl.BlockSpec((None, 1, blk), lambda b, h: (h, 0, 0))],
        out_specs=pl.BlockSpec((t_len, GROUP_DIM), lambda b, h: (b, h)),
        out_shape=jax.ShapeDtypeStruct((n_req * t_len, heads * GROUP_DIM), BF16),
        compiler_params=_params(("parallel", "parallel")), name="moba_prompt",
    )(proj, proj, proj, own, prev, far)


def _cache_logits_kernel(pt_ref, *refs, pages):
    k_refs, qbd_ref, s_ref, g_ref = refs[:pages], refs[pages], refs[pages + 1], refs[pages + 2]
    keys = jnp.concatenate([r[...].astype(BF16) for r in k_refs], axis=0)
    s = jnp.dot(keys, qbd_ref[...].astype(BF16), preferred_element_type=F32)
    s_ref[...] = s
    for n in range(g_ref.shape[0]):
        g_ref[n] = jnp.mean(s[n * MOBA_BLOCK:(n + 1) * MOBA_BLOCK], axis=0, keepdims=True)


def _cache_select_kernel(s_ref, g_ref, knew_ref, qbd_ref, blast_ref, far_ref, bown_ref, mown_ref,
                         p_ref, pown_ref, *, nblk):
    blk = MOBA_BLOCK
    scale = GROUP_DIM ** -0.5
    sel = _top_blocks(g_ref[...], nblk, nblk)
    lo = jnp.dot(knew_ref[...].astype(BF16), qbd_ref[...].astype(BF16), preferred_element_type=F32)
    lo = jnp.where(mown_ref[...] > 0.0, lo * scale + bown_ref[...], NEG_BIG)
    far = far_ref[...]

    def logits(j):
        bias = blast_ref[...] if j == nblk - 1 else far
        return jnp.where(sel[j:j + 1, :] > 0.0, s_ref[j * blk:(j + 1) * blk, :] * scale + bias, NEG_BIG)

    mx = jnp.max(lo, axis=0, keepdims=True)
    for j in range(nblk):
        mx = jnp.maximum(mx, jnp.max(logits(j), axis=0, keepdims=True))
    den = jnp.sum(jnp.exp(lo - mx), axis=0, keepdims=True)
    for j in range(nblk):
        den = den + jnp.sum(jnp.exp(logits(j) - mx), axis=0, keepdims=True)
    inv = 1.0 / den
    pown_ref[...] = jnp.exp(lo - mx) * inv
    for j in range(nblk):
        p_ref[j * blk:(j + 1) * blk, :] = jnp.exp(logits(j) - mx) * inv


def _cache_pv_kernel(pt_ref, *refs, pages):
    v_refs, p_ref, pown_ref, vnew_ref, o_ref, acc_s = (refs[:pages], refs[pages], refs[pages + 1],
                                                       refs[pages + 2], refs[pages + 3], refs[pages + 4])
    j = pl.program_id(1)
    ta = (((0,), (0,)), ((), ()))

    @pl.when(j == 0)
    def _():
        acc_s[...] = lax.dot_general(pown_ref[...].astype(BF16), vnew_ref[...].astype(BF16), ta,
                                     preferred_element_type=F32)

    vals = jnp.concatenate([r[...].astype(BF16) for r in v_refs], axis=0)
    acc_s[...] += lax.dot_general(p_ref[...].astype(BF16), vals, ta, preferred_element_type=F32)

    @pl.when(j == pl.num_programs(1) - 1)
    def _():
        o_ref[...] = acc_s[...]


def _moba_sample(proj_s, col0, cache_k, cache_v, l, page_table, rel_bias, *, heads, t_valid):
    n_req, n_pages = page_table.shape
    depth, n_phys, psz = cache_k.shape[:3]
    width = heads * GROUP_DIM
    assert col0 % width == 0
    lanes = heads * 8
    assert t_valid <= 8 and lanes == GROUP_DIM
    past = n_pages * psz
    blk = MOBA_BLOCK
    nblk = past // blk
    assert past % blk == 0 and t_valid <= blk
    bps = 2 if nblk % 2 == 0 else 1
    pps = bps * blk // psz
    nstep = nblk // bps
    rows = SAMPLE_ROWS

    q = proj_s[:, col0:col0 + width].reshape(n_req, rows, heads, GROUP_DIM)[:, :8]
    eye = jnp.eye(heads, dtype=F32)
    qbd = jnp.einsum("bthd,hg->bhdgt", q, eye).reshape(n_req, width, lanes)
    ck = cache_k.reshape(depth, n_phys, psz, width)
    cv = cache_v.reshape(depth, n_phys, psz, width)
    pt = page_table.reshape(-1).astype(jnp.int32)

    page = lambda p: pl.BlockSpec((None, None, psz, width),
                                  lambda b, j, pt_ref: (l, pt_ref[b * n_pages + j * pps + p], 0, 0))
    s_all, gate = pl.pallas_call(
        functools.partial(_cache_logits_kernel, pages=pps),
        grid_spec=pltpu.PrefetchScalarGridSpec(
            num_scalar_prefetch=1, grid=(n_req, nstep),
            in_specs=[page(p) for p in range(pps)] + [pl.BlockSpec((None, width, lanes), lambda b, j, pt_ref: (b, 0, 0))],
            out_specs=[pl.BlockSpec((None, bps * blk, lanes), lambda b, j, pt_ref: (b, j, 0)),
                       pl.BlockSpec((None, bps, 1, lanes), lambda b, j, pt_ref: (b, j, 0, 0))]),
        out_shape=[jax.ShapeDtypeStruct((n_req, past, lanes), F32),
                   jax.ShapeDtypeStruct((n_req, nblk, 1, lanes), F32)],
        compiler_params=_params(("parallel", "arbitrary")), name="cache_logits",
    )(pt, *([ck] * pps), qbd)

    tq = np.arange(lanes) % 8
    hq = np.arange(lanes) // 8
    srow = np.arange(blk)[:, None]
    b_last = _bias_lookup(rel_bias, _t5_bucket_np(blk + tq[None, :] - srow), hq)
    assert (_t5_bucket_np(np.array([blk + 1, past + 8])) == N_BUCKETS - 1).all()
    far = _bias_lookup(rel_bias, np.full((1, lanes), N_BUCKETS - 1), hq)
    orow = np.arange(rows)[:, None]
    b_own = _bias_lookup(rel_bias, _t5_bucket_np(tq[None, :] - orow), hq)
    m_own = jnp.asarray(((orow <= tq[None, :]) & (orow < t_valid)).astype(np.float32))

    kcol = col0 + width
    vcol = col0 + 2 * width
    full2 = lambda b: (0, 0)
    probs, p_own = pl.pallas_call(
        functools.partial(_cache_select_kernel, nblk=nblk),
        grid=(n_req,),
        in_specs=[pl.BlockSpec((None, past, lanes), lambda b: (b, 0, 0)),
                  pl.BlockSpec((None, nblk, lanes), lambda b: (b, 0, 0)),
                  pl.BlockSpec((rows, width), lambda b: (b, kcol // width)),
                  pl.BlockSpec((None, width, lanes), lambda b: (b, 0, 0)),
                  pl.BlockSpec((blk, lanes), full2), pl.BlockSpec((1, lanes), full2),
                  pl.BlockSpec((rows, lanes), full2), pl.BlockSpec((rows, lanes), full2)],
        out_specs=[pl.BlockSpec((None, past, lanes), lambda b: (b, 0, 0)),
                   pl.BlockSpec((None, rows, lanes), lambda b: (b, 0, 0))],
        out_shape=[jax.ShapeDtypeStruct((n_req, past, lanes), F32),
                   jax.ShapeDtypeStruct((n_req, rows, lanes), F32)],
        compiler_params=_params(("parallel",)), name="cache_select",
    )(s_all, gate.reshape(n_req, nblk, lanes), proj_s, qbd, b_last, far, b_own, m_own)

    out_full = pl.pallas_call(
        functools.partial(_cache_pv_kernel, pages=pps),
        grid_spec=pltpu.PrefetchScalarGridSpec(
            num_scalar_prefetch=1, grid=(n_req, nstep),
            in_specs=[page(p) for p in range(pps)]
            + [pl.BlockSpec((None, bps * blk, lanes), lambda b, j, pt_ref: (b, j, 0)),
               pl.BlockSpec((None, rows, lanes), lambda b, j, pt_ref: (b, 0, 0)),
               pl.BlockSpec((rows, width), lambda b, j, pt_ref: (b, vcol // width))],
            out_specs=pl.BlockSpec((None, lanes, width), lambda b, j, pt_ref: (b, 0, 0)),
            scratch_shapes=[pltpu.VMEM((lanes, width), F32)]),
        out_shape=jax.ShapeDtypeStruct((n_req, lanes, width), F32),
        compiler_params=_params(("parallel", "arbitrary")), name="cache_pv",
    )(pt, *([cv] * pps), probs, p_own, proj_s)

    o = out_full.reshape(n_req, heads, 8, heads, GROUP_DIM)
    o = jnp.stack([o[:, h, :, h, :] for h in range(heads)], axis=2)
    o = jnp.pad(o, ((0, 0), (0, rows - 8), (0, 0), (0, 0)))
    return o.reshape(n_req * rows, width).astype(BF16)


def _merge_kernel(oa_ref, ob_ref, oc_ref, wa_ref, wb_ref, wc_ref, ga_ref, gb_ref, gc_ref, o_ref):
    def branch(o_r, w_r, g_r):
        y = jnp.dot(o_r[...], w_r[...].astype(BF16), preferred_element_type=F32)
        return _sigmoid(g_r[...]) * y
    o_ref[...] = (branch(oa_ref, wa_ref, ga_ref) + branch(ob_ref, wb_ref, gb_ref)
                  + branch(oc_ref, wc_ref, gc_ref)).astype(o_ref.dtype)


def _merge(o_a, o_b, o_c, w_a, w_b, w_c, l, proj, gcol0, *, tm=1024, tn=512):
    m = o_a.shape[0]
    d = w_a.shape[2]
    tm = min(tm, m)
    assert gcol0 % tn == 0 and d % tn == 0
    gbase = gcol0 // tn
    nd = d // tn
    act = lambda o: pl.BlockSpec((tm, o.shape[1]), lambda i, j: (i, 0))
    wsp = lambda w: pl.BlockSpec((None, w.shape[1], tn), lambda i, j: (l, 0, j))
    gsp = lambda br: pl.BlockSpec((tm, tn), lambda i, j: (i, gbase + br * nd + j))
    return pl.pallas_call(
        _merge_kernel, grid=(m // tm, d // tn),
        in_specs=[act(o_a), act(o_b), act(o_c), wsp(w_a), wsp(w_b), wsp(w_c), gsp(0), gsp(1), gsp(2)],
        out_specs=pl.BlockSpec((tm, tn), lambda i, j: (i, j)),
        out_shape=jax.ShapeDtypeStruct((m, d), BF16),
        compiler_params=_params(("parallel", "parallel")), name="merge",
    )(o_a, o_b, o_c, w_a, w_b, w_c, proj, proj, proj)


def _staircase():
    return [(r1, r2) for r1 in range(PEER_TOPK) for r2 in range(PEER_TOPK) if (r1 + 1) * (r2 + 1) <= PEER_TOPK]


def _top_values(s, count):
    rid = lax.broadcasted_iota(jnp.int32, s.shape, 0)
    big = s.shape[0]
    out = []
    for _ in range(count):
        mx = jnp.max(s, axis=0, keepdims=True)
        out.append(mx)
        idx = jnp.min(jnp.where(s == mx, rid, big), axis=0, keepdims=True)
        s = jnp.where(rid == idx, -jnp.inf, s)
    return out


LOG2E = 1.4426950408889634


def _peer_retrieve_kernel(q_ref, k1_ref, k2_ref, a1_ref, a2_ref, tau_ref, sh_ref, *, heads, half):
    nt = (((1,), (1,)), ((), ()))
    pairs = _staircase()
    for h in range(heads):
        q1 = q_ref[:, (2 * h) * half:(2 * h + 1) * half]
        q2 = q_ref[:, (2 * h + 1) * half:(2 * h + 2) * half]
        a1 = lax.dot_general(k1_ref[h], q1, nt, preferred_element_type=F32) * LOG2E
        a2 = lax.dot_general(k2_ref[h], q2, nt, preferred_element_type=F32) * LOG2E
        v1 = _top_values(a1, PEER_TOPK)
        v2 = _top_values(a2, PEER_TOPK)
        fill = [jnp.full(v1[0].shape, -jnp.inf, F32)] * ((-len(pairs)) % 8)
        cand = jnp.concatenate([v1[a] + v2[b] for a, b in pairs] + fill, axis=0)
        tau = _top_values(cand, PEER_TOPK)[-1]
        top = v1[0] + v2[0]
        z = jnp.sum(jnp.where(cand >= tau, jnp.exp2(cand - top), 0.0), axis=0, keepdims=True)
        a1_ref[h] = a1
        a2_ref[h] = a2
        tau_ref[h] = tau
        sh_ref[h] = top + jnp.log2(z)


def _peer_retrieve(q, keys1, keys2, *, tm=256):
    m = q.shape[0]
    heads, nkeys, half = keys1.shape
    tm = min(tm, m)
    tok = pl.BlockSpec((heads, nkeys, tm), lambda i: (0, 0, i))
    row = pl.BlockSpec((heads, 1, tm), lambda i: (0, 0, i))
    keyspec = pl.BlockSpec((heads, nkeys, half), lambda i: (0, 0, 0))
    big = jax.ShapeDtypeStruct((heads, nkeys, m), F32)
    small = jax.ShapeDtypeStruct((heads, 1, m), F32)
    return pl.pallas_call(
        functools.partial(_peer_retrieve_kernel, heads=heads, half=half),
        grid=(m // tm,),
        in_specs=[pl.BlockSpec((tm, q.shape[1]), lambda i: (i, 0)), keyspec, keyspec],
        out_specs=[tok, tok, row, row],
        out_shape=[big, big, small, small],
        compiler_params=_params(("parallel",)), name="peer_retrieve",
    )(q, keys1, keys2)


def _peer_dense_kernel(h_ref, u_ref, v_ref, a1_ref, a2_ref, tau_ref, sh_ref, y_ref, *, heads, nsub, nkeys):
    j = pl.program_id(1)
    at = lax.dot_general(u_ref[...], h_ref[...], (((1,), (1,)), ((), ())), preferred_element_type=F32)
    act = _gelu(at)
    parts = []
    for a in range(nsub):
        w = None
        row = pl.ds(j * nsub + a, 1)
        for h in range(heads):
            sg = a1_ref[h, row, :] + a2_ref[h]
            term = jnp.where(sg >= tau_ref[h], jnp.exp2(sg - sh_ref[h]), 0.0)
            w = term if w is None else w + term
        parts.append((w * act[a * nkeys:(a + 1) * nkeys, :]).astype(BF16))
    pt = jnp.concatenate(parts, axis=0) if nsub > 1 else parts[0]

    @pl.when(j == 0)
    def _():
        y_ref[...] = jnp.zeros(y_ref.shape, F32)

    y_ref[...] += lax.dot_general(pt, v_ref[...], (((0,), (0,)), ((), ())), preferred_element_type=F32)


def _peer_dense(h, u_tab, v_tab, l, a1, a2, tau, sh, *, tm=512, tn=512):
    m, d = h.shape
    n_exp = u_tab.shape[1]
    heads, nkeys, _ = a1.shape
    tm = min(tm, m)
    nsub = tn // nkeys
    once = dict(pipeline_mode=pl.Buffered(1))
    tok = pl.BlockSpec((heads, nkeys, tm), lambda i, j: (0, 0, i), **once)
    row = pl.BlockSpec((heads, 1, tm), lambda i, j: (0, 0, i), **once)
    return pl.pallas_call(
        functools.partial(_peer_dense_kernel, heads=heads, nsub=nsub, nkeys=nkeys),
        grid=(m // tm, n_exp // tn),
        in_specs=[pl.BlockSpec((tm, d), lambda i, j: (i, 0), **once),
                  pl.BlockSpec((None, tn, d), lambda i, j: (l, j, 0)),
                  pl.BlockSpec((None, tn, d), lambda i, j: (l, j, 0)),
                  tok, tok, row, row],
        out_specs=pl.BlockSpec((tm, d), lambda i, j: (i, 0), **once),
        out_shape=jax.ShapeDtypeStruct((m, d), F32),
        compiler_params=_params(("parallel", "arbitrary")), name="peer_dense",
    )(h, u_tab, v_tab, a1, a2, tau, sh)


def _split_mod(mod, d, per_row):
    parts = [mod[:, k * d:(k + 1) * d] for k in range(6)]
    if per_row is None:
        return [p[:, None, :] for p in parts]
    return [jnp.repeat(p, per_row, axis=0)[None] for p in parts]


def _spatial_weights(w_s, b_s, *, t_valid, width):
    groups, chunk, _ = w_s.shape
    if t_valid is None:
        w = jnp.where(np.tril(np.ones((chunk, chunk), bool))[None], w_s, 0.0)
        b = b_s.T
    else:
        n_rep = chunk // SAMPLE_ROWS
        small = jnp.where(np.tril(np.ones((t_valid, t_valid), bool))[None], w_s[:, :t_valid, :t_valid], 0.0)
        small = jnp.pad(small, ((0, 0), (0, SAMPLE_ROWS - t_valid), (0, SAMPLE_ROWS - t_valid)))
        w = jnp.einsum("rs,gab->grasb", jnp.eye(n_rep, dtype=F32), small).reshape(groups, chunk, chunk)
        b = jnp.tile(jnp.pad(b_s[:, :t_valid].T, ((0, SAMPLE_ROWS - t_valid), (0, 0))), (n_rep, 1))
    return w, jnp.repeat(b, width // groups, axis=1)


def _layer(p, l, lb, x, pending, mod, s0, attn_fn, *, n_req, t_len, t_valid, rows_per_group, hgrn_chunk, dims):
    d, a_w, b_w, c_w = dims
    sh1, sc1, g1, sh2, sc2, g2 = mod
    y_prev, gate_prev = pending if pending is not None else (None, None)
    x_new, h = _rownorm(x, p["norm_mix"][l], sc=sc1, sh=sh1, y=y_prev, gate=gate_prev,
                        rows_per_group=rows_per_group, out_dtype=BF16)
    x = x if pending is None else x_new
    proj = _matmul(h, p["w_in"], l)
    col_b = 2 * a_w
    col_c = col_b + 4 * b_w
    col_g = col_c + 3 * c_w
    w_eff, b_eff = _spatial_weights(p["w_spatial"][l], p["b_spatial"][l], t_valid=t_valid, width=a_w)
    res_a = _branch_a(proj, w_eff, b_eff, p["ln_v_gain"][l], p["ln_v_bias"][l], width=a_w,
                      want_av=t_valid is not None)
    o_a = res_a[0]
    a_v = res_a[1] if t_valid is not None else None
    o_b, s_fin = _hgrn(proj, col_b, lb, s0, p["hgrn_norm"][l], n_req=n_req, t_len=t_len, t_valid=t_valid,
                       chunk=hgrn_chunk)
    o_c = attn_fn(proj, col_c)
    merged = _merge(o_a, o_b, o_c, p["w_branch_a"], p["w_branch_b"], p["w_branch_c"], l, proj, col_g)
    x1 = _matmul(merged, p["w_out"], l, resid=x, gate=g1, rows_per_group=rows_per_group)
    _, h2 = _rownorm(x1, p["norm_ffn"][l], sc=sc2, sh=sh2, rows_per_group=rows_per_group, out_dtype=BF16)
    q = _matmul(h2, p["peer_wq"], l)
    a1, a2, tau, shift = _peer_retrieve(q, p["peer_keys1"][l], p["peer_keys2"][l])
    y = _peer_dense(h2, p["peer_u16"], p["peer_v16"], l, a1, a2, tau, shift)
    k_new = proj[:, col_c + c_w:col_c + 2 * c_w]
    v_new = proj[:, col_c + 2 * c_w:col_c + 3 * c_w]
    return x1, (y, g2), a_v, k_new, v_new, s_fin


def kernel(x_prompt, x_sample, cache_k, cache_v, page_table, state_hgrn, c_prompt, c_sample, w_ada, b_ada, norm_mix, norm_ffn, norm_final, w_in, ln_v_gain, ln_v_bias, w_spatial, b_spatial, hgrn_gamma, hgrn_norm, rel_bias, w_branch_a, w_branch_b, w_branch_c, w_out, peer_wq, peer_keys1, peer_keys2, peer_u, peer_v):
    depth = w_in.shape[0]
    n_p, t_p, d = x_prompt.shape
    n_s, t_s, _ = x_sample.shape
    a_w = ln_v_gain.shape[1]
    b_w = hgrn_gamma.shape[1]
    heads_b = b_w // GROUP_DIM
    heads_c = cache_k.shape[3]
    c_w = heads_c * GROUP_DIM
    dims = (d, a_w, b_w, c_w)
    rows = SAMPLE_ROWS

    p = dict(norm_mix=norm_mix, norm_ffn=norm_ffn, w_in=w_in, ln_v_gain=ln_v_gain, ln_v_bias=ln_v_bias,
             w_spatial=w_spatial, b_spatial=b_spatial, hgrn_norm=hgrn_norm, w_branch_a=w_branch_a,
             w_branch_b=w_branch_b, w_branch_c=w_branch_c, w_out=w_out, peer_wq=peer_wq,
             peer_keys1=peer_keys1, peer_keys2=peer_keys2,
             peer_u16=peer_u.astype(BF16), peer_v16=peer_v.astype(BF16))

    gam = jax.nn.softmax(hgrn_gamma.astype(F32), axis=0)
    lower_bounds = (jnp.cumsum(gam, axis=0) - gam[:1]).reshape(depth, heads_b, GROUP_DIM)

    xp = x_prompt.reshape(n_p * t_p, d)
    xs = jnp.pad(x_sample, ((0, 0), (0, rows - t_s), (0, 0))).reshape(n_s * rows, d)
    c_all = jnp.pad(jnp.concatenate([c_prompt, c_sample], axis=0), ((0, (-(n_p + n_s)) % 8), (0, 0)))
    s0_prompt = jnp.zeros((n_p, heads_b, GROUP_DIM, GROUP_DIM), F32)
    bias_tiles = _moba_bias_tiles(rel_bias)

    pend_p = pend_s = None
    outs = {k: [] for k in ("kp", "vp", "ks", "vs", "hp", "hs", "av")}
    for l in range(depth):
        mod = _matmul(c_all, w_ada, l, bias=b_ada, pre_silu=True, tn=1024)
        mod_p = _split_mod(mod[:n_p], d, None)
        mod_s = _split_mod(mod[n_p:n_p + n_s], d, rows)
        attn_p = lambda proj, col: _moba_prompt(proj, col, bias_tiles, n_req=n_p, t_len=t_p, heads=heads_c)
        xp, pend_p, _, k_p, v_p, s_p = _layer(
            p, l, lower_bounds[l], xp, pend_p, mod_p, s0_prompt, attn_p, n_req=n_p, t_len=t_p, t_valid=None,
            rows_per_group=t_p, hgrn_chunk=min(256, t_p), dims=dims)
        attn_s = lambda proj, col: _moba_sample(proj, col, cache_k, cache_v, l, page_table, rel_bias,
                                                heads=heads_c, t_valid=t_s)
        xs, pend_s, av_s, k_s, v_s, s_s = _layer(
            p, l, lower_bounds[l], xs, pend_s, mod_s, state_hgrn[l], attn_s, n_req=n_s, t_len=rows, t_valid=t_s,
            rows_per_group=None, hgrn_chunk=rows, dims=dims)
        outs["kp"].append(k_p.reshape(n_p, t_p, heads_c, GROUP_DIM))
        outs["vp"].append(v_p.reshape(n_p, t_p, heads_c, GROUP_DIM))
        outs["ks"].append(k_s.reshape(n_s, rows, heads_c, GROUP_DIM)[:, :t_s])
        outs["vs"].append(v_s.reshape(n_s, rows, heads_c, GROUP_DIM)[:, :t_s])
        outs["hp"].append(s_p)
        outs["hs"].append(s_s)
        outs["av"].append(av_s.reshape(n_s, rows, a_w)[:, :t_s])

    _, y_prompt = _rownorm(xp, norm_final, y=pend_p[0], gate=pend_p[1], rows_per_group=t_p, out_dtype=F32)
    _, y_sample = _rownorm(xs, norm_final, y=pend_s[0], gate=pend_s[1], rows_per_group=None, out_dtype=F32)
    y_prompt = y_prompt.reshape(n_p, t_p, d)
    y_sample = y_sample.reshape(n_s, rows, d)[:, :t_s]
    st = jnp.stack
    return (y_prompt, y_sample, st(outs["kp"]), st(outs["vp"]), st(outs["ks"]), st(outs["vs"]),
            st(outs["hp"]), st(outs["hs"]), st(outs["av"]))
```

```python
import functools
import math

import numpy as np
import jax
import jax.numpy as jnp
from jax import lax
from jax.experimental import pallas as pl
from jax.experimental.pallas import tpu as pltpu

F32 = jnp.float32
BF16 = jnp.bfloat16

EPS = 1e-6
NEG_BIG = -1e30
LB_FLOOR = 1e-30

GROUP_DIM = 128
MOBA_BLOCK = 256
MOBA_TOPK = 3
N_BUCKETS = 32
MAX_DISTANCE = 128
PEER_TOPK = 16
HGRN_SUB = 16
SAMPLE_ROWS = 16

VMEM_LIMIT_V7X = 56 * 1024 * 1024


def _params(semantics):
    return pltpu.CompilerParams(dimension_semantics=semantics, vmem_limit_bytes=VMEM_LIMIT_V7X)


def _gelu(x):
    return jax.nn.gelu(x)


def _sigmoid(x):
    return jax.nn.sigmoid(x)


def _group_spec(arr, tm, rows_per_group, ncol_block, col_of):
    g, r, _ = arr.shape
    if r == 1:
        return pl.BlockSpec((None, 1, ncol_block), lambda *ix: ((ix[0] * tm) // rows_per_group, 0, col_of(*ix)))
    assert g == 1
    return pl.BlockSpec((None, tm, ncol_block), lambda *ix: (0, ix[0], col_of(*ix)))


def _rownorm_kernel(*refs, has_res, has_mod):
    it = iter(refs)
    x_ref = next(it)
    if has_res:
        y_ref = next(it)
        g_ref = next(it)
    gain_ref = next(it)
    if has_mod:
        sc_ref = next(it)
        sh_ref = next(it)
    if has_res:
        xo_ref = next(it)
    h_ref = next(it)
    x = x_ref[...]
    if has_res:
        x = x + g_ref[...] * y_ref[...]
        xo_ref[...] = x
    h = x * lax.rsqrt(jnp.mean(x * x, axis=-1, keepdims=True) + EPS) * gain_ref[...]
    if has_mod:
        h = h * (1.0 + sc_ref[...]) + sh_ref[...]
    h_ref[...] = h.astype(h_ref.dtype)


def _rownorm(x, gain, *, sc=None, sh=None, y=None, gate=None, rows_per_group, out_dtype, tm=256):
    m, d = x.shape
    tm = min(tm, m)
    has_res, has_mod = y is not None, sc is not None
    row = pl.BlockSpec((tm, d), lambda i: (i, 0))
    zero = lambda i: 0
    ins, specs = [x], [row]
    if has_res:
        ins += [y, gate]
        specs += [row, _group_spec(gate, tm, rows_per_group, d, zero)]
    ins.append(gain.reshape(1, d))
    specs.append(pl.BlockSpec((1, d), lambda i: (0, 0)))
    if has_mod:
        ins += [sc, sh]
        specs += [_group_spec(sc, tm, rows_per_group, d, zero), _group_spec(sh, tm, rows_per_group, d, zero)]
    out_shape, out_specs = [], []
    if has_res:
        out_shape.append(jax.ShapeDtypeStruct((m, d), F32))
        out_specs.append(row)
    out_shape.append(jax.ShapeDtypeStruct((m, d), out_dtype))
    out_specs.append(row)
    res = pl.pallas_call(
        functools.partial(_rownorm_kernel, has_res=has_res, has_mod=has_mod),
        grid=(m // tm,), in_specs=specs, out_specs=out_specs, out_shape=out_shape,
        compiler_params=_params(("parallel",)), name="rownorm",
    )(*ins)
    return (res[0], res[1]) if has_res else (None, res[0])


def _mm_kernel(*refs, nk, pre_silu, has_bias, has_res):
    it = iter(refs)
    x_ref = next(it)
    w_ref = next(it)
    b_ref = next(it) if has_bias else None
    if has_res:
        r_ref = next(it)
        g_ref = next(it)
    o_ref = next(it)
    acc_ref = next(it) if nk > 1 else None
    x = x_ref[...]
    if pre_silu:
        x = x * _sigmoid(x)
    part = jnp.dot(x.astype(BF16), w_ref[...].astype(BF16), preferred_element_type=F32)

    def finish(acc):
        if has_bias:
            acc = acc + b_ref[...]
        if has_res:
            acc = r_ref[...] + g_ref[...] * acc
        o_ref[...] = acc.astype(o_ref.dtype)

    if nk == 1:
        finish(part)
    else:
        k = pl.program_id(2)

        @pl.when(k == 0)
        def _():
            acc_ref[...] = part

        @pl.when(k > 0)
        def _():
            acc_ref[...] += part

        @pl.when(k == nk - 1)
        def _():
            finish(acc_ref[...])


def _matmul(x, w, l, *, bias=None, resid=None, gate=None, rows_per_group=None, pre_silu=False,
            out_dtype=F32, tm=1024, tn=512, tk=None):
    m, kdim = x.shape
    n = w.shape[2]
    tm, tn = min(tm, m), min(tn, n)
    tk = kdim if tk is None else min(tk, kdim)
    nk = kdim // tk
    has_bias, has_res = bias is not None, resid is not None
    ins = [x, w]
    specs = [pl.BlockSpec((tm, tk), lambda i, j, k: (i, k)),
             pl.BlockSpec((None, tk, tn), lambda i, j, k: (l, k, j))]
    if has_bias:
        ins.append(bias.reshape(bias.shape[0], 1, n))
        specs.append(pl.BlockSpec((None, 1, tn), lambda i, j, k: (l, 0, j)))
    if has_res:
        ins += [resid, gate]
        specs += [pl.BlockSpec((tm, tn), lambda i, j, k: (i, j)),
                  _group_spec(gate, tm, rows_per_group, tn, lambda i, j, k: j)]
    return pl.pallas_call(
        functools.partial(_mm_kernel, nk=nk, pre_silu=pre_silu, has_bias=has_bias, has_res=has_res),
        grid=(m // tm, n // tn, nk), in_specs=specs,
        out_specs=pl.BlockSpec((tm, tn), lambda i, j, k: (i, j)),
        out_shape=jax.ShapeDtypeStruct((m, n), out_dtype),
        scratch_shapes=[pltpu.VMEM((tm, tn), F32)] if nk > 1 else [],
        compiler_params=_params(("parallel", "parallel", "arbitrary")), name="matmul",
    )(*ins)


def _branch_a_kernel(u_ref, v_ref, w_ref, b_ref, lg_ref, lb_ref, o_ref, *av_ref, groups):
    u = _gelu(u_ref[...])
    v = _gelu(v_ref[...])
    mu = jnp.mean(v, axis=-1, keepdims=True)
    vc = v - mu
    var = jnp.mean(vc * vc, axis=-1, keepdims=True)
    v = vc * lax.rsqrt(var + EPS) * lg_ref[...] + lb_ref[...]
    if av_ref:
        av_ref[0][...] = v
    vb = v.astype(BF16)
    for g in range(groups):
        cols = slice(g * GROUP_DIM, (g + 1) * GROUP_DIM)
        mixed = jnp.dot(w_ref[g].astype(BF16), vb[:, cols], preferred_element_type=F32) + b_ref[:, cols]
        o_ref[:, cols] = (u[:, cols] * mixed).astype(o_ref.dtype)


def _branch_a(proj, w_eff, b_eff, ln_gain, ln_bias, *, width, want_av):
    m = proj.shape[0]
    groups, chunk, _ = w_eff.shape
    full = lambda i: (0, 0)
    out_shape = [jax.ShapeDtypeStruct((m, width), BF16)]
    out_specs = [pl.BlockSpec((chunk, width), lambda i: (i, 0))]
    if want_av:
        out_shape.append(jax.ShapeDtypeStruct((m, width), F32))
        out_specs.append(pl.BlockSpec((chunk, width), lambda i: (i, 0)))
    return pl.pallas_call(
        functools.partial(_branch_a_kernel, groups=groups),
        grid=(m // chunk,),
        in_specs=[pl.BlockSpec((chunk, width), lambda i: (i, 0)), pl.BlockSpec((chunk, width), lambda i: (i, 1)),
                  pl.BlockSpec((groups, chunk, chunk), lambda i: (0, 0, 0)), pl.BlockSpec((chunk, width), full),
                  pl.BlockSpec((1, width), full), pl.BlockSpec((1, width), full)],
        out_specs=out_specs, out_shape=out_shape,
        compiler_params=_params(("parallel",)), name="branch_a",
    )(proj, proj, w_eff, b_eff, ln_gain.reshape(1, width), ln_bias.reshape(1, width))


def _hgrn_kernel(q_ref, f_ref, i_ref, g_ref, lb_ref, s0_ref, gain_ref, o_ref, sfin_ref,
                 st_s, q_s, k_s, b_s, o_s, *, chunk, t_valid):
    c = pl.program_id(2)

    @pl.when(c == 0)
    def _():
        st_s[...] = s0_ref[...].T

    z = f_ref[...]
    lb = lb_ref[...]
    soft = jnp.log1p(jnp.exp(-jnp.abs(z)))
    ls_pos = jnp.minimum(z, 0.0) - soft
    ls_neg = jnp.minimum(-z, 0.0) - soft
    t2 = jnp.log(jnp.maximum(lb, LB_FLOOR)) + ls_neg
    log_f = jnp.maximum(ls_pos, t2) + jnp.log1p(jnp.exp(-jnp.abs(ls_pos - t2)))
    k_in = (1.0 - lb) * _sigmoid(-z)
    row = lax.broadcasted_iota(jnp.int32, z.shape, 0)
    sub = row % HGRN_SUB
    if t_valid is not None:
        live = row < t_valid
        log_f = jnp.where(live, log_f, 0.0)
        k_in = jnp.where(live, k_in, 0.0)
    b = log_f
    step = 1
    while step < HGRN_SUB:
        b = b + jnp.where(sub >= step, pltpu.roll(b, step, 0), 0.0)
        step *= 2
    qv = q_ref[...]
    q_s[...] = qv * _sigmoid(qv)
    k_s[...] = k_in
    b_s[...] = b

    ti = lax.broadcasted_iota(jnp.int32, (HGRN_SUB, HGRN_SUB, GROUP_DIM), 0)
    si = lax.broadcasted_iota(jnp.int32, (HGRN_SUB, HGRN_SUB, GROUP_DIM), 1)
    causal = si <= ti
    ones = jnp.ones((GROUP_DIM, GROUP_DIM), BF16)

    st = st_s[...]
    for j in range(chunk // HGRN_SUB):
        r = slice(j * HGRN_SUB, (j + 1) * HGRN_SUB)
        qb, kb, bb, ib = q_s[r, :], k_s[r, :], b_s[r, :], i_ref[r, :]
        bl = bb[HGRN_SUB - 1:HGRN_SUB, :]
        inter = lax.dot_general(qb * jnp.exp(bb), st, (((1,), (1,)), ((), ())), preferred_element_type=F32)
        dec = jnp.exp(jnp.where(causal, bb[:, None, :] - bb[None, :, :], NEG_BIG))
        a = (dec * qb[:, None, :]) * kb[None, :, :]
        rs = jnp.dot(a.reshape(HGRN_SUB * HGRN_SUB, GROUP_DIM).astype(BF16), ones,
                     preferred_element_type=F32)
        intra = jnp.sum(rs.reshape(HGRN_SUB, HGRN_SUB, GROUP_DIM) * ib[None, :, :], axis=1)
        o_s[r, :] = inter + intra
        ke = kb * jnp.exp(bl - bb)
        upd = lax.dot_general(ib, ke, (((0,), (0,)), ((), ())), preferred_element_type=F32)
        st = st * jnp.exp(bl) + upd
    st_s[...] = st

    o = o_s[...]
    o = o * lax.rsqrt(jnp.mean(o * o, axis=-1, keepdims=True) + EPS) * gain_ref[...]
    gv = g_ref[...]
    o_ref[...] = (o * (gv * _sigmoid(gv))).astype(o_ref.dtype)

    @pl.when(c == pl.num_programs(2) - 1)
    def _():
        sfin_ref[...] = st_s[...].T


def _hgrn(proj, col0, lb, s0, norm_gain, *, n_req, t_len, t_valid, chunk):
    heads = lb.shape[0]
    width = heads * GROUP_DIM
    base = col0 // GROUP_DIM
    nch = t_len // chunk
    col = lambda off: pl.BlockSpec((chunk, GROUP_DIM), lambda b, h, c: (b * nch + c, base + off * heads + h))
    o, s_fin = pl.pallas_call(
        functools.partial(_hgrn_kernel, chunk=chunk, t_valid=t_valid),
        grid=(n_req, heads, nch),
        in_specs=[col(0), col(1), col(2), col(3),
                  pl.BlockSpec((None, 1, GROUP_DIM), lambda b, h, c: (h, 0, 0)),
                  pl.BlockSpec((None, None, GROUP_DIM, GROUP_DIM), lambda b, h, c: (b, h, 0, 0)),
                  pl.BlockSpec((1, GROUP_DIM), lambda b, h, c: (0, 0))],
        out_specs=[pl.BlockSpec((chunk, GROUP_DIM), lambda b, h, c: (b * nch + c, h)),
                   pl.BlockSpec((None, None, GROUP_DIM, GROUP_DIM), lambda b, h, c: (b, h, 0, 0))],
        out_shape=[jax.ShapeDtypeStruct((n_req * t_len, width), BF16),
                   jax.ShapeDtypeStruct((n_req, heads, GROUP_DIM, GROUP_DIM), F32)],
        scratch_shapes=[pltpu.VMEM((GROUP_DIM, GROUP_DIM), F32)] + [pltpu.VMEM((chunk, GROUP_DIM), F32)] * 4,
        compiler_params=_params(("parallel", "parallel", "arbitrary")), name="hgrn",
    )(proj, proj, proj, proj, lb.reshape(heads, 1, GROUP_DIM), s0, norm_gain.reshape(1, GROUP_DIM))
    return o, s_fin


def _t5_bucket_np(rel):
    n = np.maximum(rel, 0)
    max_exact = N_BUCKETS // 2
    nf = np.maximum(n, max_exact).astype(np.float32)
    large = max_exact + (np.log(nf / np.float32(max_exact)) / np.float32(math.log(MAX_DISTANCE / max_exact))
                         * np.float32(N_BUCKETS - max_exact)).astype(np.int32)
    large = np.minimum(large, N_BUCKETS - 1)
    return np.where(n < max_exact, n, large).astype(np.int32)


def _top_blocks(gate, n_past, n_keep):
    nblk = gate.shape[0]
    rid = lax.broadcasted_iota(jnp.int32, gate.shape, 0)
    g = jnp.where(rid < n_past, gate, NEG_BIG)
    sel = jnp.zeros(gate.shape, F32)
    for r in range(min(MOBA_TOPK, nblk)):
        mx = jnp.max(g, axis=0, keepdims=True)
        idx = jnp.min(jnp.where(g == mx, rid, nblk), axis=0, keepdims=True)
        pick = rid == idx
        live = jnp.logical_and(pick, jnp.full(gate.shape, r, jnp.int32) < n_keep)
        sel = jnp.where(live, 1.0, sel)
        g = jnp.where(pick, -jnp.inf, g)
    return sel


def _moba_prompt_kernel(q_ref, k_ref, v_ref, bo_ref, bp_ref, far_ref, o_ref, *, nblk):
    blk = MOBA_BLOCK
    scale = GROUP_DIM ** -0.5
    nt = (((1,), (1,)), ((), ()))
    kb = k_ref[...].astype(BF16)
    vb = v_ref[...].astype(BF16)
    k_mean = jnp.concatenate(
        [jnp.mean(k_ref[j * blk:(j + 1) * blk, :], axis=0, keepdims=True) for j in range(nblk)], axis=0)
    ks = lax.broadcasted_iota(jnp.int32, (blk, blk), 0)
    qs = lax.broadcasted_iota(jnp.int32, (blk, blk), 1)
    causal = ks <= qs
    for i in range(nblk):
        rows = slice(i * blk, (i + 1) * blk)
        n = (i + 1) * blk
        qf = q_ref[rows, :]
        s = lax.dot_general(kb[:n], qf.astype(BF16), nt, preferred_element_type=F32) * scale
        if i > 0:
            gate = lax.dot_general(k_mean, qf, nt, preferred_element_type=F32)
            sel = _top_blocks(gate, i, i)
        parts = []
        for j in range(i + 1):
            sj = s[j * blk:(j + 1) * blk]
            if j == i:
                sj = jnp.where(causal, sj + bo_ref[...], NEG_BIG)
            else:
                bias = bp_ref[...] if j == i - 1 else far_ref[...]
                sj = jnp.where(sel[j:j + 1, :] > 0.0, sj + bias, NEG_BIG)
            parts.append(sj)
        mx = jnp.max(parts[0], axis=0, keepdims=True)
        for sj in parts[1:]:
            mx = jnp.maximum(mx, jnp.max(sj, axis=0, keepdims=True))
        probs = [jnp.exp(sj - mx) for sj in parts]
        den = jnp.sum(probs[0], axis=0, keepdims=True)
        for pj in probs[1:]:
            den = den + jnp.sum(pj, axis=0, keepdims=True)
        p = (jnp.concatenate(probs, axis=0) if i > 0 else probs[0]).astype(BF16)
        acc = lax.dot_general(vb[:n], p, (((0,), (0,)), ((), ())), preferred_element_type=F32)
        o_ref[rows, :] = (acc / den).T.astype(o_ref.dtype)


def _moba_bias_tiles(rel_bias):
    blk = MOBA_BLOCK
    s = np.arange(blk)[:, None]
    t = np.arange(blk)[None, :]
    own = jnp.transpose(_bias_lookup(rel_bias, _t5_bucket_np(t - s)), (2, 0, 1))
    prev = jnp.transpose(_bias_lookup(rel_bias, _t5_bucket_np(blk + t - s)), (2, 0, 1))
    far_bucket = _t5_bucket_np(np.array([blk + 1, 2 * blk, 1 << 20]))
    assert far_bucket.min() == far_bucket.max() == N_BUCKETS - 1
    far = jnp.broadcast_to(rel_bias[N_BUCKETS - 1][:, None, None], (rel_bias.shape[1], 1, blk))
    return own, prev, far


def _bias_lookup(rel_bias, buckets, heads=None):
    hot = (jnp.asarray(buckets)[..., None] == jnp.arange(N_BUCKETS)).astype(F32)
    vals = jnp.einsum("...n,nh->...h", hot, rel_bias, precision=lax.Precision.HIGHEST)
    if heads is None:
        return vals
    pick = (jnp.asarray(heads)[:, None] == jnp.arange(rel_bias.shape[1])).astype(F32)
    return jnp.einsum("...lh,lh->...l", vals, pick, precision=lax.Precision.HIGHEST)


def _moba_prompt(proj, col0, bias_tiles, *, n_req, t_len, heads):
    blk = MOBA_BLOCK
    nblk = t_len // blk
    base = col0 // GROUP_DIM
    own, prev, far = bias_tiles
    seq = lambda off: pl.BlockSpec((t_len, GROUP_DIM), lambda b, h: (b, base + off * heads + h))
    tile = pl.BlockSpec((None, blk, blk), lambda b, h: (h, 0, 0))
    return pl.pallas_call(
        functools.partial(_moba_prompt_kernel, nblk=nblk),
        grid=(n_req, heads),
        in_specs=[seq(0), seq(1), seq(2), tile, tile, pl.BlockSpec((None, 1, blk), lambda b, h: (h, 0, 0))],
        out_specs=pl.BlockSpec((t_len, GROUP_DIM), lambda b, h: (b, h)),
        out_shape=jax.ShapeDtypeStruct((n_req * t_len, heads * GROUP_DIM), BF16),
        compiler_params=_params(("parallel", "parallel")), name="moba_prompt",
    )(proj, proj, proj, own, prev, far)


def _page_rows(page_ref, heads):
    psz = page_ref.shape[0] // heads
    return jnp.concatenate([page_ref[pl.ds(h, psz, stride=heads), :] for h in range(heads)], axis=1).astype(BF16)


def _cache_logits_kernel(pt_ref, *refs, pages, heads):
    k_refs, qbd_ref, s_ref, g_ref = refs[:pages], refs[pages], refs[pages + 1], refs[pages + 2]
    keys = jnp.concatenate([_page_rows(r, heads) for r in k_refs], axis=0)
    s = jnp.dot(keys, qbd_ref[...].astype(BF16), preferred_element_type=F32)
    s_ref[...] = s
    for n in range(g_ref.shape[0]):
        g_ref[n] = jnp.mean(s[n * MOBA_BLOCK:(n + 1) * MOBA_BLOCK], axis=0, keepdims=True)


def _cache_select_kernel(s_ref, g_ref, knew_ref, qbd_ref, blast_ref, far_ref, bown_ref, mown_ref,
                         p_ref, pown_ref, *, nblk):
    blk = MOBA_BLOCK
    scale = GROUP_DIM ** -0.5
    sel = _top_blocks(g_ref[...], nblk, nblk)
    lo = jnp.dot(knew_ref[...].astype(BF16), qbd_ref[...].astype(BF16), preferred_element_type=F32)
    lo = jnp.where(mown_ref[...] > 0.0, lo * scale + bown_ref[...], NEG_BIG)
    far = far_ref[...]

    def logits(j):
        bias = blast_ref[...] if j == nblk - 1 else far
        return jnp.where(sel[j:j + 1, :] > 0.0, s_ref[j * blk:(j + 1) * blk, :] * scale + bias, NEG_BIG)

    mx = jnp.max(lo, axis=0, keepdims=True)
    for j in range(nblk):
        mx = jnp.maximum(mx, jnp.max(logits(j), axis=0, keepdims=True))
    den = jnp.sum(jnp.exp(lo - mx), axis=0, keepdims=True)
    for j in range(nblk):
        den = den + jnp.sum(jnp.exp(logits(j) - mx), axis=0, keepdims=True)
    inv = 1.0 / den
    pown_ref[...] = jnp.exp(lo - mx) * inv
    for j in range(nblk):
        p_ref[j * blk:(j + 1) * blk, :] = jnp.exp(logits(j) - mx) * inv


def _cache_pv_kernel(pt_ref, *refs, pages, heads):
    v_refs, p_ref, pown_ref, vnew_ref, o_ref, acc_s = (refs[:pages], refs[pages], refs[pages + 1],
                                                       refs[pages + 2], refs[pages + 3], refs[pages + 4])
    j = pl.program_id(1)
    ta = (((0,), (0,)), ((), ()))

    @pl.when(j == 0)
    def _():
        acc_s[...] = lax.dot_general(pown_ref[...].astype(BF16), vnew_ref[...].astype(BF16), ta,
                                     preferred_element_type=F32)

    vals = jnp.concatenate([_page_rows(r, heads) for r in v_refs], axis=0)
    acc_s[...] += lax.dot_general(p_ref[...].astype(BF16), vals, ta, preferred_element_type=F32)

    @pl.when(j == pl.num_programs(1) - 1)
    def _():
        o_ref[...] = acc_s[...]


def _moba_sample(proj_s, col0, cache_k, cache_v, l, page_table, rel_bias, *, heads, t_valid):
    n_req, n_pages = page_table.shape
    depth, n_phys, psz = cache_k.shape[:3]
    width = heads * GROUP_DIM
    assert col0 % width == 0
    lanes = heads * 8
    assert t_valid <= 8 and lanes == GROUP_DIM
    past = n_pages * psz
    blk = MOBA_BLOCK
    nblk = past // blk
    assert past % blk == 0 and t_valid <= blk
    bps = 2 if nblk % 2 == 0 else 1
    pps = bps * blk // psz
    nstep = nblk // bps
    rows = SAMPLE_ROWS

    q = proj_s[:, col0:col0 + width].reshape(n_req, rows, heads, GROUP_DIM)[:, :8]
    eye = jnp.eye(heads, dtype=F32)
    qbd = jnp.einsum("bthd,hg->bhdgt", q, eye).reshape(n_req, width, lanes)
    ck = cache_k.reshape(depth, n_phys, psz * heads, GROUP_DIM)
    cv = cache_v.reshape(depth, n_phys, psz * heads, GROUP_DIM)
    pt = page_table.reshape(-1).astype(jnp.int32)

    page = lambda p: pl.BlockSpec((None, None, psz * heads, GROUP_DIM),
                                  lambda b, j, pt_ref: (l, pt_ref[b * n_pages + j * pps + p], 0, 0))
    s_all, gate = pl.pallas_call(
        functools.partial(_cache_logits_kernel, pages=pps, heads=heads),
        grid_spec=pltpu.PrefetchScalarGridSpec(
            num_scalar_prefetch=1, grid=(n_req, nstep),
            in_specs=[page(p) for p in range(pps)] + [pl.BlockSpec((None, width, lanes), lambda b, j, pt_ref: (b, 0, 0))],
            out_specs=[pl.BlockSpec((None, bps * blk, lanes), lambda b, j, pt_ref: (b, j, 0)),
                       pl.BlockSpec((None, bps, 1, lanes), lambda b, j, pt_ref: (b, j, 0, 0))]),
        out_shape=[jax.ShapeDtypeStruct((n_req, past, lanes), F32),
                   jax.ShapeDtypeStruct((n_req, nblk, 1, lanes), F32)],
        compiler_params=_params(("parallel", "arbitrary")), name="cache_logits",
    )(pt, *([ck] * pps), qbd)

    tq = np.arange(lanes) % 8
    hq = np.arange(lanes) // 8
    srow = np.arange(blk)[:, None]
    b_last = _bias_lookup(rel_bias, _t5_bucket_np(blk + tq[None, :] - srow), hq)
    assert (_t5_bucket_np(np.array([blk + 1, past + 8])) == N_BUCKETS - 1).all()
    far = _bias_lookup(rel_bias, np.full((1, lanes), N_BUCKETS - 1), hq)
    orow = np.arange(rows)[:, None]
    b_own = _bias_lookup(rel_bias, _t5_bucket_np(tq[None, :] - orow), hq)
    m_own = jnp.asarray(((orow <= tq[None, :]) & (orow < t_valid)).astype(np.float32))

    kcol = col0 + width
    vcol = col0 + 2 * width
    full2 = lambda b: (0, 0)
    probs, p_own = pl.pallas_call(
        functools.partial(_cache_select_kernel, nblk=nblk),
        grid=(n_req,),
        in_specs=[pl.BlockSpec((None, past, lanes), lambda b: (b, 0, 0)),
                  pl.BlockSpec((None, nblk, lanes), lambda b: (b, 0, 0)),
                  pl.BlockSpec((rows, width), lambda b: (b, kcol // width)),
                  pl.BlockSpec((None, width, lanes), lambda b: (b, 0, 0)),
                  pl.BlockSpec((blk, lanes), full2), pl.BlockSpec((1, lanes), full2),
                  pl.BlockSpec((rows, lanes), full2), pl.BlockSpec((rows, lanes), full2)],
        out_specs=[pl.BlockSpec((None, past, lanes), lambda b: (b, 0, 0)),
                   pl.BlockSpec((None, rows, lanes), lambda b: (b, 0, 0))],
        out_shape=[jax.ShapeDtypeStruct((n_req, past, lanes), F32),
                   jax.ShapeDtypeStruct((n_req, rows, lanes), F32)],
        compiler_params=_params(("parallel",)), name="cache_select",
    )(s_all, gate.reshape(n_req, nblk, lanes), proj_s, qbd, b_last, far, b_own, m_own)

    out_full = pl.pallas_call(
        functools.partial(_cache_pv_kernel, pages=pps, heads=heads),
        grid_spec=pltpu.PrefetchScalarGridSpec(
            num_scalar_prefetch=1, grid=(n_req, nstep),
            in_specs=[page(p) for p in range(pps)]
            + [pl.BlockSpec((None, bps * blk, lanes), lambda b, j, pt_ref: (b, j, 0)),
               pl.BlockSpec((None, rows, lanes), lambda b, j, pt_ref: (b, 0, 0)),
               pl.BlockSpec((rows, width), lambda b, j, pt_ref: (b, vcol // width))],
            out_specs=pl.BlockSpec((None, lanes, width), lambda b, j, pt_ref: (b, 0, 0)),
            scratch_shapes=[pltpu.VMEM((lanes, width), F32)]),
        out_shape=jax.ShapeDtypeStruct((n_req, lanes, width), F32),
        compiler_params=_params(("parallel", "arbitrary")), name="cache_pv",
    )(pt, *([cv] * pps), probs, p_own, proj_s)

    o = out_full.reshape(n_req, heads, 8, heads, GROUP_DIM)
    o = jnp.stack([o[:, h, :, h, :] for h in range(heads)], axis=2)
    o = jnp.pad(o, ((0, 0), (0, rows - 8), (0, 0), (0, 0)))
    return o.reshape(n_req * rows, width).astype(BF16)


def _merge_kernel(oa_ref, ob_ref, oc_ref, wa_ref, wb_ref, wc_ref, ga_ref, gb_ref, gc_ref, o_ref):
    def branch(o_r, w_r, g_r):
        y = jnp.dot(o_r[...], w_r[...].astype(BF16), preferred_element_type=F32)
        return _sigmoid(g_r[...]) * y
    o_ref[...] = (branch(oa_ref, wa_ref, ga_ref) + branch(ob_ref, wb_ref, gb_ref)
                  + branch(oc_ref, wc_ref, gc_ref)).astype(o_ref.dtype)


def _merge(o_a, o_b, o_c, w_a, w_b, w_c, l, proj, gcol0, *, tm=1024, tn=512):
    m = o_a.shape[0]
    d = w_a.shape[2]
    tm = min(tm, m)
    assert gcol0 % tn == 0 and d % tn == 0
    gbase = gcol0 // tn
    nd = d // tn
    act = lambda o: pl.BlockSpec((tm, o.shape[1]), lambda i, j: (i, 0))
    wsp = lambda w: pl.BlockSpec((None, w.shape[1], tn), lambda i, j: (l, 0, j))
    gsp = lambda br: pl.BlockSpec((tm, tn), lambda i, j: (i, gbase + br * nd + j))
    return pl.pallas_call(
        _merge_kernel, grid=(m // tm, d // tn),
        in_specs=[act(o_a), act(o_b), act(o_c), wsp(w_a), wsp(w_b), wsp(w_c), gsp(0), gsp(1), gsp(2)],
        out_specs=pl.BlockSpec((tm, tn), lambda i, j: (i, j)),
        out_shape=jax.ShapeDtypeStruct((m, d), BF16),
        compiler_params=_params(("parallel", "parallel")), name="merge",
    )(o_a, o_b, o_c, w_a, w_b, w_c, proj, proj, proj)


def _staircase():
    return [(r1, r2) for r1 in range(PEER_TOPK) for r2 in range(PEER_TOPK) if (r1 + 1) * (r2 + 1) <= PEER_TOPK]


def _top_values(s, count):
    rid = lax.broadcasted_iota(jnp.int32, s.shape, 0)
    big = s.shape[0]
    out = []
    for _ in range(count):
        mx = jnp.max(s, axis=0, keepdims=True)
        out.append(mx)
        idx = jnp.min(jnp.where(s == mx, rid, big), axis=0, keepdims=True)
        s = jnp.where(rid == idx, -jnp.inf, s)
    return out


LOG2E = 1.4426950408889634


def _peer_retrieve_kernel(q_ref, k1_ref, k2_ref, a1_ref, a2_ref, tau_ref, sh_ref, *, heads, half):
    nt = (((1,), (1,)), ((), ()))
    pairs = _staircase()
    for h in range(heads):
        q1 = q_ref[:, (2 * h) * half:(2 * h + 1) * half]
        q2 = q_ref[:, (2 * h + 1) * half:(2 * h + 2) * half]
        a1 = lax.dot_general(k1_ref[h], q1, nt, preferred_element_type=F32) * LOG2E
        a2 = lax.dot_general(k2_ref[h], q2, nt, preferred_element_type=F32) * LOG2E
        v1 = _top_values(a1, PEER_TOPK)
        v2 = _top_values(a2, PEER_TOPK)
        fill = [jnp.full(v1[0].shape, -jnp.inf, F32)] * ((-len(pairs)) % 8)
        cand = jnp.concatenate([v1[a] + v2[b] for a, b in pairs] + fill, axis=0)
        tau = _top_values(cand, PEER_TOPK)[-1]
        top = v1[0] + v2[0]
        z = jnp.sum(jnp.where(cand >= tau, jnp.exp2(cand - top), 0.0), axis=0, keepdims=True)
        a1_ref[h] = a1
        a2_ref[h] = a2
        tau_ref[h] = tau
        sh_ref[h] = top + jnp.log2(z)


def _peer_retrieve(q, keys1, keys2, *, tm=256):
    m = q.shape[0]
    heads, nkeys, half = keys1.shape
    tm = min(tm, m)
    tok = pl.BlockSpec((heads, nkeys, tm), lambda i: (0, 0, i))
    row = pl.BlockSpec((heads, 1, tm), lambda i: (0, 0, i))
    keyspec = pl.BlockSpec((heads, nkeys, half), lambda i: (0, 0, 0))
    big = jax.ShapeDtypeStruct((heads, nkeys, m), F32)
    small = jax.ShapeDtypeStruct((heads, 1, m), F32)
    return pl.pallas_call(
        functools.partial(_peer_retrieve_kernel, heads=heads, half=half),
        grid=(m // tm,),
        in_specs=[pl.BlockSpec((tm, q.shape[1]), lambda i: (i, 0)), keyspec, keyspec],
        out_specs=[tok, tok, row, row],
        out_shape=[big, big, small, small],
        compiler_params=_params(("parallel",)), name="peer_retrieve",
    )(q, keys1, keys2)


def _peer_dense_kernel(h_ref, u_ref, v_ref, a1_ref, a2_ref, tau_ref, sh_ref, y_ref, *, heads, nsub, nkeys):
    j = pl.program_id(1)
    at = lax.dot_general(u_ref[...], h_ref[...], (((1,), (1,)), ((), ())), preferred_element_type=F32)
    act = _gelu(at)
    parts = []
    for a in range(nsub):
        w = None
        row = pl.ds(j * nsub + a, 1)
        for h in range(heads):
            sg = a1_ref[h, row, :] + a2_ref[h]
            term = jnp.where(sg >= tau_ref[h], jnp.exp2(sg - sh_ref[h]), 0.0)
            w = term if w is None else w + term
        parts.append((w * act[a * nkeys:(a + 1) * nkeys, :]).astype(BF16))
    pt = jnp.concatenate(parts, axis=0) if nsub > 1 else parts[0]

    @pl.when(j == 0)
    def _():
        y_ref[...] = jnp.zeros(y_ref.shape, F32)

    y_ref[...] += lax.dot_general(pt, v_ref[...], (((0,), (0,)), ((), ())), preferred_element_type=F32)


def _peer_dense(h, u_tab, v_tab, l, a1, a2, tau, sh, *, tm=512, tn=512):
    m, d = h.shape
    n_exp = u_tab.shape[1]
    heads, nkeys, _ = a1.shape
    tm = min(tm, m)
    nsub = tn // nkeys
    once = dict(pipeline_mode=pl.Buffered(1))
    tok = pl.BlockSpec((heads, nkeys, tm), lambda i, j: (0, 0, i), **once)
    row = pl.BlockSpec((heads, 1, tm), lambda i, j: (0, 0, i), **once)
    return pl.pallas_call(
        functools.partial(_peer_dense_kernel, heads=heads, nsub=nsub, nkeys=nkeys),
        grid=(m // tm, n_exp // tn),
        in_specs=[pl.BlockSpec((tm, d), lambda i, j: (i, 0), **once),
                  pl.BlockSpec((None, tn, d), lambda i, j: (l, j, 0)),
                  pl.BlockSpec((None, tn, d), lambda i, j: (l, j, 0)),
                  tok, tok, row, row],
        out_specs=pl.BlockSpec((tm, d), lambda i, j: (i, 0), **once),
        out_shape=jax.ShapeDtypeStruct((m, d), F32),
        compiler_params=_params(("parallel", "arbitrary")), name="peer_dense",
    )(h, u_tab, v_tab, a1, a2, tau, sh)


def _split_mod(mod, d, per_row):
    parts = [mod[:, k * d:(k + 1) * d] for k in range(6)]
    if per_row is None:
        return [p[:, None, :] for p in parts]
    return [jnp.repeat(p, per_row, axis=0)[None] for p in parts]


def _spatial_weights(w_s, b_s, *, t_valid, width):
    groups, chunk, _ = w_s.shape
    if t_valid is None:
        w = jnp.where(np.tril(np.ones((chunk, chunk), bool))[None], w_s, 0.0)
        b = b_s.T
    else:
        n_rep = chunk // SAMPLE_ROWS
        small = jnp.where(np.tril(np.ones((t_valid, t_valid), bool))[None], w_s[:, :t_valid, :t_valid], 0.0)
        small = jnp.pad(small, ((0, 0), (0, SAMPLE_ROWS - t_valid), (0, SAMPLE_ROWS - t_valid)))
        w = jnp.einsum("rs,gab->grasb", jnp.eye(n_rep, dtype=F32), small).reshape(groups, chunk, chunk)
        b = jnp.tile(jnp.pad(b_s[:, :t_valid].T, ((0, SAMPLE_ROWS - t_valid), (0, 0))), (n_rep, 1))
    return w, jnp.repeat(b, width // groups, axis=1)


def _layer(p, l, lb, x, pending, mod, s0, attn_fn, *, n_req, t_len, t_valid, rows_per_group, hgrn_chunk, dims):
    d, a_w, b_w, c_w = dims
    sh1, sc1, g1, sh2, sc2, g2 = mod
    y_prev, gate_prev = pending if pending is not None else (None, None)
    x_new, h = _rownorm(x, p["norm_mix"][l], sc=sc1, sh=sh1, y=y_prev, gate=gate_prev,
                        rows_per_group=rows_per_group, out_dtype=BF16)
    x = x if pending is None else x_new
    proj = _matmul(h, p["w_in"], l)
    col_b = 2 * a_w
    col_c = col_b + 4 * b_w
    col_g = col_c + 3 * c_w
    w_eff, b_eff = _spatial_weights(p["w_spatial"][l], p["b_spatial"][l], t_valid=t_valid, width=a_w)
    res_a = _branch_a(proj, w_eff, b_eff, p["ln_v_gain"][l], p["ln_v_bias"][l], width=a_w,
                      want_av=t_valid is not None)
    o_a = res_a[0]
    a_v = res_a[1] if t_valid is not None else None
    o_b, s_fin = _hgrn(proj, col_b, lb, s0, p["hgrn_norm"][l], n_req=n_req, t_len=t_len, t_valid=t_valid,
                       chunk=hgrn_chunk)
    o_c = attn_fn(proj, col_c)
    merged = _merge(o_a, o_b, o_c, p["w_branch_a"], p["w_branch_b"], p["w_branch_c"], l, proj, col_g)
    x1 = _matmul(merged, p["w_out"], l, resid=x, gate=g1, rows_per_group=rows_per_group)
    _, h2 = _rownorm(x1, p["norm_ffn"][l], sc=sc2, sh=sh2, rows_per_group=rows_per_group, out_dtype=BF16)
    q = _matmul(h2, p["peer_wq"], l)
    a1, a2, tau, shift = _peer_retrieve(q, p["peer_keys1"][l], p["peer_keys2"][l])
    y = _peer_dense(h2, p["peer_u16"], p["peer_v16"], l, a1, a2, tau, shift)
    k_new = proj[:, col_c + c_w:col_c + 2 * c_w]
    v_new = proj[:, col_c + 2 * c_w:col_c + 3 * c_w]
    return x1, (y, g2), a_v, k_new, v_new, s_fin


def kernel(x_prompt, x_sample, cache_k, cache_v, page_table, state_hgrn, c_prompt, c_sample, w_ada, b_ada, norm_mix, norm_ffn, norm_final, w_in, ln_v_gain, ln_v_bias, w_spatial, b_spatial, hgrn_gamma, hgrn_norm, rel_bias, w_branch_a, w_branch_b, w_branch_c, w_out, peer_wq, peer_keys1, peer_keys2, peer_u, peer_v):
    depth = w_in.shape[0]
    n_p, t_p, d = x_prompt.shape
    n_s, t_s, _ = x_sample.shape
    a_w = ln_v_gain.shape[1]
    b_w = hgrn_gamma.shape[1]
    heads_b = b_w // GROUP_DIM
    heads_c = cache_k.shape[3]
    c_w = heads_c * GROUP_DIM
    dims = (d, a_w, b_w, c_w)
    rows = SAMPLE_ROWS

    p = dict(norm_mix=norm_mix, norm_ffn=norm_ffn, w_in=w_in, ln_v_gain=ln_v_gain, ln_v_bias=ln_v_bias,
             w_spatial=w_spatial, b_spatial=b_spatial, hgrn_norm=hgrn_norm, w_branch_a=w_branch_a,
             w_branch_b=w_branch_b, w_branch_c=w_branch_c, w_out=w_out, peer_wq=peer_wq,
             peer_keys1=peer_keys1, peer_keys2=peer_keys2,
             peer_u16=peer_u.astype(BF16), peer_v16=peer_v.astype(BF16))

    gam = jax.nn.softmax(hgrn_gamma.astype(F32), axis=0)
    lower_bounds = (jnp.cumsum(gam, axis=0) - gam[:1]).reshape(depth, heads_b, GROUP_DIM)

    xp = x_prompt.reshape(n_p * t_p, d)
    xs = jnp.pad(x_sample, ((0, 0), (0, rows - t_s), (0, 0))).reshape(n_s * rows, d)
    c_all = jnp.pad(jnp.concatenate([c_prompt, c_sample], axis=0), ((0, (-(n_p + n_s)) % 8), (0, 0)))
    s0_prompt = jnp.zeros((n_p, heads_b, GROUP_DIM, GROUP_DIM), F32)
    bias_tiles = _moba_bias_tiles(rel_bias)

    pend_p = pend_s = None
    outs = {k: [] for k in ("kp", "vp", "ks", "vs", "hp", "hs", "av")}
    for l in range(depth):
        mod = _matmul(c_all, w_ada, l, bias=b_ada, pre_silu=True, tn=1024)
        mod_p = _split_mod(mod[:n_p], d, None)
        mod_s = _split_mod(mod[n_p:n_p + n_s], d, rows)
        attn_p = lambda proj, col: _moba_prompt(proj, col, bias_tiles, n_req=n_p, t_len=t_p, heads=heads_c)
        xp, pend_p, _, k_p, v_p, s_p = _layer(
            p, l, lower_bounds[l], xp, pend_p, mod_p, s0_prompt, attn_p, n_req=n_p, t_len=t_p, t_valid=None,
            rows_per_group=t_p, hgrn_chunk=min(256, t_p), dims=dims)
        attn_s = lambda proj, col: _moba_sample(proj, col, cache_k, cache_v, l, page_table, rel_bias,
                                                heads=heads_c, t_valid=t_s)
        xs, pend_s, av_s, k_s, v_s, s_s = _layer(
            p, l, lower_bounds[l], xs, pend_s, mod_s, state_hgrn[l], attn_s, n_req=n_s, t_len=rows, t_valid=t_s,
            rows_per_group=None, hgrn_chunk=rows, dims=dims)
        outs["kp"].append(k_p.reshape(n_p, t_p, heads_c, GROUP_DIM))
        outs["vp"].append(v_p.reshape(n_p, t_p, heads_c, GROUP_DIM))
        outs["ks"].append(k_s.reshape(n_s, rows, heads_c, GROUP_DIM)[:, :t_s])
        outs["vs"].append(v_s.reshape(n_s, rows, heads_c, GROUP_DIM)[:, :t_s])
        outs["hp"].append(s_p)
        outs["hs"].append(s_s)
        outs["av"].append(av_s.reshape(n_s, rows, a_w)[:, :t_s])

    _, y_prompt = _rownorm(xp, norm_final, y=pend_p[0], gate=pend_p[1], rows_per_group=t_p, out_dtype=F32)
    _, y_sample = _rownorm(xs, norm_final, y=pend_s[0], gate=pend_s[1], rows_per_group=None, out_dtype=F32)
    y_prompt = y_prompt.reshape(n_p, t_p, d)
    y_sample = y_sample.reshape(n_s, rows, d)[:, :t_s]
    st = jnp.stack
    return (y_prompt, y_sample, st(outs["kp"]), st(outs["vp"]), st(outs["ks"]), st(outs["vs"]),
            st(outs["hp"]), st(outs["hs"]), st(outs["av"]))
```
